```python
import jax, jax.numpy as jnp
from jax import lax
import numpy as np

D_MODEL = 1024
BATCH = 4
SEQ = 8192
DEPTH = 1

SSD_HEADS = 16
SSD_HEAD_DIM = 64
SSD_WIDTH = SSD_HEADS * SSD_HEAD_DIM
SSD_GROUPS = 2
SSD_STATE = 128
SSD_CONV = 4
SSD_XBC_WIDTH = SSD_WIDTH + 2 * SSD_GROUPS * SSD_STATE
RET_HEADS = 8
RET_QK_DIM = 64
RET_V_DIM = 128
RET_QK_WIDTH = RET_HEADS * RET_QK_DIM
RET_V_WIDTH = RET_HEADS * RET_V_DIM
ROPE_BASE = 10000.0
CHUNK = 128
MIX_WIDTH = SSD_WIDTH + RET_V_WIDTH
IN_WIDTH = SSD_WIDTH + SSD_XBC_WIDTH + SSD_HEADS + 2 * RET_QK_WIDTH + 2 * RET_V_WIDTH
D_FF = 2816
FFN_CONV = 3
EPS = 1e-6

kernel_name = "hymba_ssd_retention_convffn_block"


def rmsnorm(x, w):
    xf = x.astype(jnp.float32)
    y = xf * lax.rsqrt(jnp.mean(xf * xf, axis=-1, keepdims=True) + EPS)
    return (y * w.astype(jnp.float32)).astype(x.dtype)


def causal_dwconv(x, w, b):
    k = w.shape[0]
    y = lax.conv_general_dilated(
        x, w[:, None, :].astype(x.dtype), window_strides=(1,), padding=[(k - 1, 0)],
        dimension_numbers=("NWC", "WIO", "NWC"), feature_group_count=x.shape[-1])
    return y + b.astype(x.dtype)


def rotary(t):
    s, d = t.shape[1], t.shape[-1]
    inv = ROPE_BASE ** (-jnp.arange(0, d, 2, dtype=jnp.float32) / d)
    ang = jnp.arange(s, dtype=jnp.float32)[:, None] * inv[None, :]
    cos = jnp.cos(ang)[None, :, None, :]
    sin = jnp.sin(ang)[None, :, None, :]
    tf = t.astype(jnp.float32)
    t1, t2 = jnp.split(tf, 2, axis=-1)
    return jnp.concatenate([t1 * cos - t2 * sin, t2 * cos + t1 * sin], axis=-1).astype(t.dtype)


def ssd_mixer(z, xbc, dt_raw, conv_w, conv_b, dt_bias, a_log, d_skip, norm_w):
    b, s, _ = z.shape
    nc = s // CHUNK
    hpg = SSD_HEADS // SSD_GROUPS
    xbc = jax.nn.silu(causal_dwconv(xbc, conv_w, conv_b))
    xs, bm, cm = jnp.split(xbc, [SSD_WIDTH, SSD_WIDTH + SSD_GROUPS * SSD_STATE], axis=-1)
    xs = xs.reshape(b, nc, CHUNK, SSD_GROUPS, hpg, SSD_HEAD_DIM)
    bm = bm.reshape(b, nc, CHUNK, SSD_GROUPS, SSD_STATE)
    cm = cm.reshape(b, nc, CHUNK, SSD_GROUPS, SSD_STATE)
    dt = jax.nn.softplus(dt_raw.astype(jnp.float32) + dt_bias.astype(jnp.float32))
    a = -jnp.exp(a_log.astype(jnp.float32))
    dt_c = dt.reshape(b, nc, CHUNK, SSD_GROUPS, hpg)
    da = jnp.transpose((dt * a).reshape(b, nc, CHUNK, SSD_GROUPS, hpg), (0, 1, 3, 4, 2))
    acs = jnp.cumsum(da, axis=-1)
    xdt = xs * dt_c[..., None].astype(xs.dtype)
    causal = jnp.tril(jnp.ones((CHUNK, CHUNK), dtype=bool))
    seg = acs[..., :, None] - acs[..., None, :]
    lmat = jnp.exp(jnp.where(causal, seg, -jnp.inf)).astype(xs.dtype)
    cb = jnp.einsum("bclgn,bcsgn->bcgls", cm, bm)
    y_diag = jnp.einsum("bcgkls,bcsgkp->bclgkp", cb[:, :, :, None] * lmat, xdt)
    decay_states = jnp.exp(acs[..., -1:] - acs).astype(xs.dtype)
    states = jnp.einsum("bclgn,bcgkl,bclgkp->bcgkpn", bm, decay_states, xdt)
    chunk_decay = jnp.exp(acs[..., -1]).astype(xs.dtype)

    def step(h, inp):
        st, dec = inp
        return (h * dec[..., None, None] + st).astype(h.dtype), h

    h0 = jnp.zeros_like(states[:, 0])
    _, prev = lax.scan(step, h0, (jnp.moveaxis(states, 1, 0), jnp.moveaxis(chunk_decay, 1, 0)))
    prev = jnp.moveaxis(prev, 0, 1)
    y_off = jnp.einsum("bclgn,bcgkpn,bcgkl->bclgkp", cm, prev, jnp.exp(acs).astype(xs.dtype))
    y = y_diag + y_off + xs * d_skip.reshape(SSD_GROUPS, hpg, 1).astype(xs.dtype)
    y = y.reshape(b, s, SSD_WIDTH)
    return rmsnorm(y * jax.nn.silu(z), norm_w)


def retention_mixer(q, k, v, g, norm_w):
    b, s, _ = q.shape
    nc = s // CHUNK
    q = rotary(q.reshape(b, s, RET_HEADS, RET_QK_DIM))
    k = rotary(k.reshape(b, s, RET_HEADS, RET_QK_DIM)) * (RET_QK_DIM ** -0.5)
    v = v.reshape(b, s, RET_HEADS, RET_V_DIM)
    log_gamma = jnp.log1p(-jnp.exp2(-5.0 - jnp.arange(RET_HEADS, dtype=jnp.float32)))
    pos = jnp.arange(CHUNK, dtype=jnp.float32)
    rel = pos[:, None] - pos[None, :]
    dmask = jnp.where(rel >= 0, jnp.exp(log_gamma[:, None, None] * jnp.maximum(rel, 0.0)), 0.0).astype(q.dtype)
    qc = q.reshape(b, nc, CHUNK, RET_HEADS, RET_QK_DIM)
    kc = k.reshape(b, nc, CHUNK, RET_HEADS, RET_QK_DIM)
    vc = v.reshape(b, nc, CHUNK, RET_HEADS, RET_V_DIM)
    scores = jnp.einsum("bclhd,bcshd->bchls", qc, kc) * dmask
    inner = jnp.einsum("bchls,bcshe->bclhe", scores, vc)
    k_decay = jnp.exp(log_gamma[:, None] * (CHUNK - 1 - pos)[None, :]).astype(q.dtype)
    chunk_kv = jnp.einsum("bclhd,hl,bclhe->bchde", kc, k_decay, vc)
    chunk_gamma = jnp.exp(log_gamma * CHUNK).astype(q.dtype)

    def step(r, kv):
        return (r * chunk_gamma[:, None, None] + kv).astype(r.dtype), r

    r0 = jnp.zeros_like(chunk_kv[:, 0])
    _, prev = lax.scan(step, r0, jnp.moveaxis(chunk_kv, 1, 0))
    prev = jnp.moveaxis(prev, 0, 1)
    q_decay = jnp.exp(log_gamma[:, None] * (pos + 1.0)[None, :]).astype(q.dtype)
    cross = jnp.einsum("bclhd,hl,bchde->bclhe", qc, q_decay, prev)
    o = (inner + cross).reshape(b, s, RET_HEADS, RET_V_DIM)
    of = o.astype(jnp.float32)
    of = of * lax.rsqrt(jnp.mean(of * of, axis=-1, keepdims=True) + EPS)
    o = (of.reshape(b, s, RET_V_WIDTH) * norm_w.astype(jnp.float32)).astype(q.dtype)
    return o * jax.nn.silu(g)


def setup_inputs(seed: int = 0) -> dict:
    key = jax.random.key(seed)
    ks = jax.random.split(key, 24)
    f32 = jnp.float32

    def nrm(k, shape, scale):
        return jax.random.normal(k, shape, f32) * scale

    def gain(k, shape):
        return 1.0 + 0.02 * jax.random.normal(k, shape, f32)

    dt0 = jnp.exp(jax.random.uniform(ks[6], (DEPTH, SSD_HEADS), f32, np.log(1e-3), np.log(1e-1)))
    return {
        "x": jax.random.normal(ks[0], (BATCH, SEQ, D_MODEL), f32),
        "pre_mix_norm_w": gain(ks[1], (DEPTH, D_MODEL)),
        "w_in": nrm(ks[2], (DEPTH, D_MODEL, IN_WIDTH), D_MODEL ** -0.5),
        "ssd_conv_w": nrm(ks[3], (DEPTH, SSD_CONV, SSD_XBC_WIDTH), SSD_CONV ** -0.5),
        "ssd_conv_b": nrm(ks[4], (DEPTH, SSD_XBC_WIDTH), 0.01),
        "ssd_dt_bias": dt0 + jnp.log(-jnp.expm1(-dt0)),
        "ssd_a_log": jnp.log(jax.random.uniform(ks[7], (DEPTH, SSD_HEADS), f32, 1.0, 16.0)),
        "ssd_d": gain(ks[8], (DEPTH, SSD_HEADS)),
        "ssd_norm_w": gain(ks[9], (DEPTH, SSD_WIDTH)),
        "ret_norm_w": gain(ks[10], (DEPTH, RET_V_WIDTH)),
        "w_out": nrm(ks[11], (DEPTH, MIX_WIDTH, D_MODEL), MIX_WIDTH ** -0.5),
        "post_mix_norm_w": gain(ks[12], (DEPTH, D_MODEL)),
        "pre_ffn_norm_w": gain(ks[13], (DEPTH, D_MODEL)),
        "w_up": nrm(ks[14], (DEPTH, D_MODEL, 2 * D_FF), D_MODEL ** -0.5),
        "ffn_conv_w": nrm(ks[15], (DEPTH, FFN_CONV, D_FF), FFN_CONV ** -0.5),
        "ffn_conv_b": nrm(ks[16], (DEPTH, D_FF), 0.01),
        "w_down": nrm(ks[17], (DEPTH, D_FF, D_MODEL), D_FF ** -0.5),
        "post_ffn_norm_w": gain(ks[18], (DEPTH, D_MODEL)),
    }


def reference(x, pre_mix_norm_w, w_in, ssd_conv_w, ssd_conv_b, ssd_dt_bias, ssd_a_log, ssd_d,
              ssd_norm_w, ret_norm_w, w_out, post_mix_norm_w, pre_ffn_norm_w, w_up,
              ffn_conv_w, ffn_conv_b, w_down, post_ffn_norm_w):
    splits = np.cumsum([SSD_WIDTH, SSD_XBC_WIDTH, SSD_HEADS, RET_QK_WIDTH, RET_QK_WIDTH, RET_V_WIDTH]).tolist()
    for l in range(DEPTH):
        h = rmsnorm(x, pre_mix_norm_w[l])
        proj = h @ w_in[l]
        z, xbc, dt_raw, q, k, v, g = jnp.split(proj, splits, axis=-1)
        y_ssd = ssd_mixer(z, xbc, dt_raw, ssd_conv_w[l], ssd_conv_b[l], ssd_dt_bias[l],
                          ssd_a_log[l], ssd_d[l], ssd_norm_w[l])
        y_ret = retention_mixer(q, k, v, g, ret_norm_w[l])
        y = jnp.concatenate([y_ssd, y_ret], axis=-1) @ w_out[l]
        x = x + rmsnorm(y, post_mix_norm_w[l])
        h = rmsnorm(x, pre_ffn_norm_w[l])
        gate, val = jnp.split(h @ w_up[l], 2, axis=-1)
        gate = causal_dwconv(gate, ffn_conv_w[l], ffn_conv_b[l])
        f = (jax.nn.silu(gate) * val) @ w_down[l]
        x = x + rmsnorm(f, post_ffn_norm_w[l])
    return x
```

```python
import jax
import jax.numpy as jnp
import numpy as np
from jax import lax
from jax.experimental import pallas as pl
from jax.experimental.pallas import tpu as pltpu

D_MODEL = 1024
SSD_HEADS = 16
SSD_HEAD_DIM = 64
SSD_WIDTH = SSD_HEADS * SSD_HEAD_DIM
SSD_GROUPS = 2
SSD_STATE = 128
SSD_CONV = 4
SSD_BC_WIDTH = 2 * SSD_GROUPS * SSD_STATE
SSD_XBC_WIDTH = SSD_WIDTH + SSD_BC_WIDTH
SSD_GROUP_WIDTH = SSD_WIDTH // SSD_GROUPS
RET_HEADS = 8
RET_QK_DIM = 64
RET_V_DIM = 128
RET_QK_WIDTH = RET_HEADS * RET_QK_DIM
RET_V_WIDTH = RET_HEADS * RET_V_DIM
ROPE_BASE = 10000.0
CHUNK = 128
MIX_WIDTH = SSD_WIDTH + RET_V_WIDTH
D_FF = 2816
FFN_CONV = 3
EPS = 1e-6

SUBLANES = 8
LANES = 128
MXU_COLS = 256
MIX_BLOCK_T = 512
FFN_BLOCK_T = 512
CONV_COL_TILE = 512
VMEM_LIMIT_BYTES = 56 * 1024 * 1024

OFF_Z = 0
OFF_XBC = OFF_Z + SSD_WIDTH
OFF_Q = OFF_XBC + SSD_XBC_WIDTH
OFF_K = OFF_Q + RET_QK_WIDTH
OFF_V = OFF_K + RET_QK_WIDTH
OFF_G = OFF_V + RET_V_WIDTH
OFF_DT = OFF_G + RET_V_WIDTH
IN_PACKED = OFF_DT + LANES

F32 = jnp.float32
BF16 = jnp.bfloat16


def _rmsnorm(x, w):
    return x * lax.rsqrt(jnp.mean(x * x, axis=-1, keepdims=True) + EPS) * w


def _silu(x):
    return x * (1.0 / (1.0 + jnp.exp(-x)))


def _softplus(x):
    return jnp.maximum(x, 0.0) + jnp.log1p(jnp.exp(-jnp.abs(x)))


def _dot(a, b):
    return jnp.dot(a, b, preferred_element_type=F32)


def _dot_nt(a, b):
    return lax.dot_general(a, b, (((1,), (1,)), ((), ())), preferred_element_type=F32)


def _dot_tn(a, b):
    return lax.dot_general(a, b, (((0,), (0,)), ((), ())), preferred_element_type=F32)


def _split_bf16(x):
    hi = x.astype(BF16)
    lo = (x - hi.astype(F32)).astype(BF16)
    return hi, lo


def _shifted_rows(carry, cur, n_shifts):
    cat = jnp.concatenate([carry, cur], axis=0)
    return [pltpu.roll(cat, s, 0)[SUBLANES:] for s in range(1, n_shifts + 1)]


def _mixer_kernel(x_ref, nw_ref, win_ref, cw_ref, cb_ref, dtb_ref, a_ref, dskip_ref,
                  snw_ref, rnw_ref, wout_ref, pw_ref, cos_ref, sin_ref, expand_ref,
                  dmask_ref, qdec_ref, kdec_ref, cgam_ref, o_ref,
                  carry_ref, z_ref, xs_ref, bc_ref, dt_ref, q_ref, k_ref, v_ref, sg_ref,
                  ymix_ref, st_ref, r_ref):
    t = x_ref.shape[1]

    @pl.when(pl.program_id(1) == 0)
    def _():
        carry_ref[...] = jnp.zeros_like(carry_ref)
        st_ref[...] = jnp.zeros_like(st_ref)
        r_ref[...] = jnp.zeros_like(r_ref)

    x = x_ref[0]
    h = _rmsnorm(x, nw_ref[...]).astype(BF16)
    z_ref[...] = _dot(h, win_ref[:, OFF_Z:OFF_Z + SSD_WIDTH])
    for j in range(SSD_XBC_WIDTH // CONV_COL_TILE):
        lo = j * CONV_COL_TILE
        cols = slice(lo, lo + CONV_COL_TILE)
        cur = _dot(h, win_ref[:, OFF_XBC + lo:OFF_XBC + lo + CONV_COL_TILE])
        d1, d2, d3 = _shifted_rows(carry_ref[:, cols], cur, SSD_CONV - 1)
        cw = cw_ref[:, cols]
        conv = (cw[3:4] * cur + cw[2:3] * d1 + cw[1:2] * d2 + cw[0:1] * d3
                + cb_ref[:, cols])
        carry_ref[:, cols] = cur[t - SUBLANES:]
        act = _silu(conv)
        if lo < SSD_WIDTH:
            xs_ref[:, cols] = act
        else:
            bc_ref[:, lo - SSD_WIDTH:lo - SSD_WIDTH + CONV_COL_TILE] = act.astype(BF16)

    lane = lax.broadcasted_iota(jnp.int32, (t, RET_QK_WIDTH), 1)
    first_half = (lane % RET_QK_DIM) < (RET_QK_DIM // 2)
    cos = jnp.concatenate([cos_ref[...]] * (RET_QK_WIDTH // LANES), axis=1)
    sin = jnp.concatenate([sin_ref[...]] * (RET_QK_WIDTH // LANES), axis=1)

    def rotary(u):
        swapped = jnp.where(first_half,
                            pltpu.roll(u, RET_QK_WIDTH - RET_QK_DIM // 2, 1),
                            pltpu.roll(u, RET_QK_DIM // 2, 1))
        return u * cos + swapped * sin

    q_ref[...] = rotary(_dot(h, win_ref[:, OFF_Q:OFF_Q + RET_QK_WIDTH]))
    k_ref[...] = rotary(_dot(h, win_ref[:, OFF_K:OFF_K + RET_QK_WIDTH])) * (RET_QK_DIM ** -0.5)
    v_ref[...] = _dot(h, win_ref[:, OFF_V:OFF_V + RET_V_WIDTH]).astype(BF16)
    sg_ref[...] = _silu(_dot(h, win_ref[:, OFF_G:OFF_G + RET_V_WIDTH]))
    dt_ref[...] = _softplus(_dot(h, win_ref[:, OFF_DT:OFF_DT + LANES]) + dtb_ref[...])

    row = lax.broadcasted_iota(jnp.int32, (CHUNK, CHUNK), 0)
    col = lax.broadcasted_iota(jnp.int32, (CHUNK, CHUNK), 1)
    causal = row >= col
    tril2 = jnp.where(jnp.concatenate([causal, causal], axis=1), 1.0, 0.0).astype(BF16)
    lane_lo = col < RET_QK_DIM
    row_lo = row < RET_QK_DIM

    def expand_heads(w):
        hi, lo = _split_bf16(w)
        return _dot(jnp.concatenate([hi, lo], axis=1), expand_ref[...])

    def chunk_body(c, carry):
        rows = pl.ds(pl.multiple_of(c * CHUNK, CHUNK), CHUNK)

        dt = dt_ref[rows, :]
        da = dt * a_ref[...]
        da_hi, da_lo = _split_bf16(da)
        acs = _dot(tril2, jnp.concatenate([da_hi, da_lo], axis=0))
        acs_last = acs[CHUNK - 1:CHUNK, :]
        decay_in = expand_heads(dt * jnp.exp(acs_last - acs))
        decay_out = expand_heads(jnp.exp(acs))
        acs_t = acs.T
        dt_t = dt.T
        ys = []
        for g in range(SSD_GROUPS):
            gcols = slice(g * SSD_GROUP_WIDTH, (g + 1) * SSD_GROUP_WIDTH)
            bm = bc_ref[rows, g * SSD_STATE:(g + 1) * SSD_STATE]
            cm = bc_ref[rows, (SSD_GROUPS + g) * SSD_STATE:(SSD_GROUPS + g + 1) * SSD_STATE]
            xs_g = xs_ref[rows, gcols]
            cb = _dot_nt(cm, bm)
            st = st_ref[g]
            y_off = _dot(cm, st.astype(BF16)) * decay_out[:, gcols]
            new = _dot_tn(bm, (xs_g * decay_in[:, gcols]).astype(BF16))
            st_ref[g] = st * decay_out[CHUNK - 1:CHUNK, gcols] + new
            y_diag = []
            for j in range(SSD_GROUP_WIDTH // LANES):
                gs = []
                for k in (g * 8 + 2 * j, g * 8 + 2 * j + 1):
                    seg = acs[:, k:k + 1] - acs_t[k:k + 1, :]
                    lmat = jnp.exp(jnp.where(causal, seg, -jnp.inf))
                    gs.append((cb * lmat * dt_t[k:k + 1, :]).astype(BF16))
                xp = xs_g[:, j * LANES:(j + 1) * LANES]
                rhs = jnp.concatenate([jnp.where(lane_lo, xp, 0.0),
                                       jnp.where(lane_lo, 0.0, xp)], axis=0).astype(BF16)
                y_diag.append(_dot(jnp.concatenate(gs, axis=1), rhs))
            y = jnp.concatenate(y_diag, axis=1) + y_off + xs_g * dskip_ref[:, gcols]
            ys.append(y * _silu(z_ref[rows, gcols]))
        y = jnp.concatenate(ys, axis=1)
        ymix_ref[rows, 0:SSD_WIDTH] = _rmsnorm(y, snw_ref[...]).astype(BF16)

        for j in range(RET_QK_WIDTH // LANES):
            cols = slice(j * LANES, (j + 1) * LANES)
            qp = q_ref[rows, cols]
            kp = k_ref[rows, cols]
            qd = qp * qdec_ref[:, cols]
            kd = (kp * kdec_ref[:, cols]).astype(BF16)
            vp = v_ref[rows, 2 * j * RET_V_DIM:(2 * j + 2) * RET_V_DIM]
            rp = r_ref[cols, :]
            rp_b = rp.astype(BF16)
            q2 = jnp.concatenate([jnp.where(lane_lo, qp, 0.0),
                                  jnp.where(lane_lo, 0.0, qp)], axis=0).astype(BF16)
            sc = _dot_nt(q2, kp.astype(BF16))
            kv = _dot_tn(kd, vp)
            r_ref[cols, :] = rp * cgam_ref[cols, :] + jnp.where(
                row_lo, kv[:, :RET_V_DIM], kv[:, RET_V_DIM:])
            for s in range(2):
                hd = 2 * j + s
                p = (sc[s * CHUNK:(s + 1) * CHUNK] * dmask_ref[hd]).astype(BF16)
                qm = jnp.where(lane_lo, qd, 0.0) if s == 0 else jnp.where(lane_lo, 0.0, qd)
                o = _dot(jnp.concatenate([p, qm.astype(BF16)], axis=1),
                         jnp.concatenate([vp[:, s * RET_V_DIM:(s + 1) * RET_V_DIM], rp_b], axis=0))
                hcols = slice(hd * RET_V_DIM, (hd + 1) * RET_V_DIM)
                on = _rmsnorm(o, rnw_ref[:, hcols]) * sg_ref[rows, hcols]
                ymix_ref[rows, SSD_WIDTH + hd * RET_V_DIM:SSD_WIDTH + (hd + 1) * RET_V_DIM] = (
                    on.astype(BF16))
        return carry

    lax.fori_loop(0, t // CHUNK, chunk_body, 0)

    y = _dot(ymix_ref[...], wout_ref[...])
    o_ref[0] = x + _rmsnorm(y, pw_ref[...])


def _ffn_kernel(x_ref, nw_ref, wup_ref, cw_ref, cb_ref, wdn_ref, pw_ref, o_ref,
                carry_ref, act_ref):
    t = x_ref.shape[1]

    @pl.when(pl.program_id(1) == 0)
    def _():
        carry_ref[...] = jnp.zeros_like(carry_ref)

    x = x_ref[0]
    h = _rmsnorm(x, nw_ref[...]).astype(BF16)
    for j in range(D_FF // MXU_COLS):
        lo = j * MXU_COLS
        cols = slice(lo, lo + MXU_COLS)
        gate = _dot(h, wup_ref[:, cols])
        val = _dot(h, wup_ref[:, D_FF + lo:D_FF + lo + MXU_COLS])
        d1, d2 = _shifted_rows(carry_ref[:, cols], gate, FFN_CONV - 1)
        cw = cw_ref[:, cols]
        conv = cw[2:3] * gate + cw[1:2] * d1 + cw[0:1] * d2 + cb_ref[:, cols]
        carry_ref[:, cols] = gate[t - SUBLANES:]
        act_ref[:, cols] = (_silu(conv) * val).astype(BF16)
    f = _dot(act_ref[...], wdn_ref[...])
    o_ref[0] = x + _rmsnorm(f, pw_ref[...])


def _resident(shape):
    return pl.BlockSpec(shape, lambda b, i: (0,) * len(shape),
                        pipeline_mode=pl.Buffered(1))


def _compiler_params():
    return pltpu.CompilerParams(dimension_semantics=("arbitrary", "arbitrary"),
                                vmem_limit_bytes=VMEM_LIMIT_BYTES)


def _retention_tables(seq):
    half = RET_QK_DIM // 2
    inv = ROPE_BASE ** (-jnp.arange(0, RET_QK_DIM, 2, dtype=F32) / RET_QK_DIM)
    ang = jnp.arange(seq, dtype=F32)[:, None] * inv[None, :]
    cos, sin = jnp.cos(ang), jnp.sin(ang)
    reps = LANES // RET_QK_DIM
    cos_t = jnp.tile(jnp.concatenate([cos, cos], axis=1), (1, reps))
    sin_t = jnp.tile(jnp.concatenate([-sin, sin], axis=1), (1, reps))
    log_gamma = jnp.log1p(-jnp.exp2(-5.0 - jnp.arange(RET_HEADS, dtype=F32)))
    pos = jnp.arange(CHUNK, dtype=F32)
    rel = pos[:, None] - pos[None, :]
    dmask = jnp.where(rel >= 0, jnp.exp(log_gamma[:, None, None] * jnp.maximum(rel, 0.0)), 0.0)
    k_decay = jnp.exp(log_gamma[:, None] * (CHUNK - 1 - pos)[None, :])
    q_decay = jnp.exp(log_gamma[:, None] * (pos + 1.0)[None, :])
    chunk_gamma = jnp.exp(log_gamma * CHUNK)
    kdec = jnp.repeat(k_decay.T, RET_QK_DIM, axis=1)
    qdec = jnp.repeat(q_decay.T, RET_QK_DIM, axis=1)
    cgam = jnp.broadcast_to(jnp.repeat(chunk_gamma, RET_QK_DIM)[:, None],
                            (RET_QK_WIDTH, RET_V_DIM))
    del half
    return cos_t, sin_t, dmask, qdec, kdec, cgam


def _pad_lanes(v):
    return jnp.pad(v, (0, LANES - v.shape[0])).reshape(1, LANES)


def _mixer(x, pre_w, w_in, conv_w, conv_b, dt_bias, a_log, d_skip, ssd_nw, ret_nw, w_out,
           post_w):
    bsz, seq, d = x.shape
    t = MIX_BLOCK_T
    wz, wxbc, wdt, wq, wk, wv, wg = jnp.split(
        w_in, np.cumsum([SSD_WIDTH, SSD_XBC_WIDTH, SSD_HEADS, RET_QK_WIDTH, RET_QK_WIDTH,
                         RET_V_WIDTH]).tolist(), axis=-1)
    w_packed = jnp.concatenate(
        [wz, wxbc, wq, wk, wv, wg, jnp.pad(wdt, ((0, 0), (0, LANES - SSD_HEADS)))],
        axis=-1).astype(BF16)
    a_neg = _pad_lanes(-jnp.exp(a_log.astype(F32)))
    dskip = jnp.repeat(d_skip.astype(F32), SSD_HEAD_DIM).reshape(1, SSD_WIDTH)
    head_of_col = jnp.arange(SSD_WIDTH) // SSD_HEAD_DIM
    expand = (jnp.arange(LANES)[:, None] == head_of_col[None, :]).astype(BF16)
    expand2 = jnp.concatenate([expand, expand], axis=0)
    cos_t, sin_t, dmask, qdec, kdec, cgam = _retention_tables(seq)

    blk = lambda w: pl.BlockSpec((1, t, w), lambda b, i: (b, i, 0))
    return pl.pallas_call(
        _mixer_kernel,
        grid=(bsz, seq // t),
        in_specs=[
            blk(d),
            _resident((1, d)),
            _resident((d, IN_PACKED)),
            _resident((SSD_CONV, SSD_XBC_WIDTH)),
            _resident((1, SSD_XBC_WIDTH)),
            _resident((1, LANES)),
            _resident((1, LANES)),
            _resident((1, SSD_WIDTH)),
            _resident((1, SSD_WIDTH)),
            _resident((1, RET_V_WIDTH)),
            _resident((MIX_WIDTH, d)),
            _resident((1, d)),
            pl.BlockSpec((t, LANES), lambda b, i: (i, 0)),
            pl.BlockSpec((t, LANES), lambda b, i: (i, 0)),
            _resident((2 * LANES, SSD_WIDTH)),
            _resident((RET_HEADS, CHUNK, CHUNK)),
            _resident((CHUNK, RET_QK_WIDTH)),
            _resident((CHUNK, RET_QK_WIDTH)),
            _resident((RET_QK_WIDTH, RET_V_DIM)),
        ],
        out_specs=blk(d),
        out_shape=jax.ShapeDtypeStruct(x.shape, F32),
        scratch_shapes=[
            pltpu.VMEM((SUBLANES, SSD_XBC_WIDTH), F32),
            pltpu.VMEM((t, SSD_WIDTH), F32),
            pltpu.VMEM((t, SSD_WIDTH), F32),
            pltpu.VMEM((t, SSD_BC_WIDTH), BF16),
            pltpu.VMEM((t, LANES), F32),
            pltpu.VMEM((t, RET_QK_WIDTH), F32),
            pltpu.VMEM((t, RET_QK_WIDTH), F32),
            pltpu.VMEM((t, RET_V_WIDTH), BF16),
            pltpu.VMEM((t, RET_V_WIDTH), F32),
            pltpu.VMEM((t, MIX_WIDTH), BF16),
            pltpu.VMEM((SSD_GROUPS, SSD_STATE, SSD_GROUP_WIDTH), F32),
            pltpu.VMEM((RET_QK_WIDTH, RET_V_DIM), F32),
        ],
        compiler_params=_compiler_params(),
        name="mixer",
    )(x, pre_w.reshape(1, d), w_packed, conv_w, conv_b.reshape(1, -1),
      _pad_lanes(dt_bias.astype(F32)), a_neg, dskip, ssd_nw.reshape(1, -1),
      ret_nw.reshape(1, -1), w_out.astype(BF16), post_w.reshape(1, d), cos_t, sin_t,
      expand2, dmask, qdec, kdec, cgam)


def _ffn(x, pre_w, w_up, conv_w, conv_b, w_down, post_w):
    bsz, seq, d = x.shape
    t = FFN_BLOCK_T
    blk = pl.BlockSpec((1, t, d), lambda b, i: (b, i, 0))
    return pl.pallas_call(
        _ffn_kernel,
        grid=(bsz, seq // t),
        in_specs=[
            blk,
            _resident((1, d)),
            _resident((d, 2 * D_FF)),
            _resident((FFN_CONV, D_FF)),
            _resident((1, D_FF)),
            _resident((D_FF, d)),
            _resident((1, d)),
        ],
        out_specs=blk,
        out_shape=jax.ShapeDtypeStruct(x.shape, F32),
        scratch_shapes=[
            pltpu.VMEM((SUBLANES, D_FF), F32),
            pltpu.VMEM((t, D_FF), BF16),
        ],
        compiler_params=_compiler_params(),
        name="conv_ffn",
    )(x, pre_w.reshape(1, d), w_up.astype(BF16), conv_w, conv_b.reshape(1, D_FF),
      w_down.astype(BF16), post_w.reshape(1, d))


def kernel(x, pre_mix_norm_w, w_in, ssd_conv_w, ssd_conv_b, ssd_dt_bias, ssd_a_log, ssd_d, ssd_norm_w, ret_norm_w, w_out, post_mix_norm_w, pre_ffn_norm_w, w_up, ffn_conv_w, ffn_conv_b, w_down, post_ffn_norm_w):
    for l in range(pre_mix_norm_w.shape[0]):
        x = _mixer(x, pre_mix_norm_w[l], w_in[l], ssd_conv_w[l], ssd_conv_b[l],
                   ssd_dt_bias[l], ssd_a_log[l], ssd_d[l], ssd_norm_w[l], ret_norm_w[l],
                   w_out[l], post_mix_norm_w[l])
        x = _ffn(x, pre_ffn_norm_w[l], w_up[l], ffn_conv_w[l], ffn_conv_b[l], w_down[l],
                 post_ffn_norm_w[l])
    return x
```

```python
import jax
import jax.numpy as jnp
import numpy as np
from jax import lax
from jax.experimental import pallas as pl
from jax.experimental.pallas import tpu as pltpu

D_MODEL = 1024
SSD_HEADS = 16
SSD_HEAD_DIM = 64
SSD_WIDTH = SSD_HEADS * SSD_HEAD_DIM
SSD_GROUPS = 2
SSD_STATE = 128
SSD_CONV = 4
SSD_BC_WIDTH = 2 * SSD_GROUPS * SSD_STATE
SSD_XBC_WIDTH = SSD_WIDTH + SSD_BC_WIDTH
SSD_GROUP_WIDTH = SSD_WIDTH // SSD_GROUPS
RET_HEADS = 8
RET_QK_DIM = 64
RET_V_DIM = 128
RET_QK_WIDTH = RET_HEADS * RET_QK_DIM
RET_V_WIDTH = RET_HEADS * RET_V_DIM
ROPE_BASE = 10000.0
CHUNK = 128
MIX_WIDTH = SSD_WIDTH + RET_V_WIDTH
D_FF = 2816
FFN_CONV = 3
EPS = 1e-6

SUBLANES = 8
LANES = 128
MXU_COLS = 256
MIX_BLOCK_T = 512
FFN_BLOCK_T = 512
CONV_COL_TILE = 512
VMEM_LIMIT_BYTES = 56 * 1024 * 1024

OFF_Z = 0
OFF_XBC = OFF_Z + SSD_WIDTH
OFF_Q = OFF_XBC + SSD_XBC_WIDTH
OFF_K = OFF_Q + RET_QK_WIDTH
OFF_V = OFF_K + RET_QK_WIDTH
OFF_G = OFF_V + RET_V_WIDTH
OFF_DT = OFF_G + RET_V_WIDTH
IN_PACKED = OFF_DT + LANES

F32 = jnp.float32
BF16 = jnp.bfloat16


def _rmsnorm(x, w):
    return x * lax.rsqrt(jnp.mean(x * x, axis=-1, keepdims=True) + EPS) * w


def _silu(x):
    return x * (1.0 / (1.0 + jnp.exp(-x)))


def _softplus(x):
    return jnp.maximum(x, 0.0) + jnp.log1p(jnp.exp(-jnp.abs(x)))


def _dot(a, b):
    return jnp.dot(a, b, preferred_element_type=F32)


def _dot_nt(a, b):
    return lax.dot_general(a, b, (((1,), (1,)), ((), ())), preferred_element_type=F32)


def _dot_tn(a, b):
    return lax.dot_general(a, b, (((0,), (0,)), ((), ())), preferred_element_type=F32)


def _split_bf16(x):
    hi = x.astype(BF16)
    lo = (x - hi.astype(F32)).astype(BF16)
    return hi, lo


def _shifted_rows(carry, cur, n_shifts):
    cat = jnp.concatenate([carry, cur], axis=0)
    return [pltpu.roll(cat, s, 0)[SUBLANES:] for s in range(1, n_shifts + 1)]


def _mixer_kernel(x_ref, nw_ref, win_ref, cw_ref, cb_ref, dtb_ref, a_ref, dskip_ref,
                  snw_ref, rnw_ref, wout_ref, pw_ref, cos_ref, sin_ref, expand_ref,
                  dmask_ref, qdec_ref, kdec_ref, cgam_ref, o_ref,
                  carry_ref, z_ref, xs_ref, bc_ref, dt_ref, q_ref, k_ref, v_ref, sg_ref,
                  ymix_ref, st_ref, r_ref):
    t = x_ref.shape[1]

    @pl.when(pl.program_id(1) == 0)
    def _():
        carry_ref[...] = jnp.zeros_like(carry_ref)
        st_ref[...] = jnp.zeros_like(st_ref)
        r_ref[...] = jnp.zeros_like(r_ref)

    x = x_ref[0]
    h = _rmsnorm(x, nw_ref[...]).astype(BF16)
    dt_ref[...] = _softplus(_dot(h, win_ref[:, OFF_DT:OFF_DT + LANES]) + dtb_ref[...])
    z_ref[...] = _dot(h, win_ref[:, OFF_Z:OFF_Z + SSD_WIDTH])
    for j in range(SSD_XBC_WIDTH // CONV_COL_TILE):
        lo = j * CONV_COL_TILE
        cols = slice(lo, lo + CONV_COL_TILE)
        cur = _dot(h, win_ref[:, OFF_XBC + lo:OFF_XBC + lo + CONV_COL_TILE])
        d1, d2, d3 = _shifted_rows(carry_ref[:, cols], cur, SSD_CONV - 1)
        cw = cw_ref[:, cols]
        conv = (cw[3:4] * cur + cw[2:3] * d1 + cw[1:2] * d2 + cw[0:1] * d3
                + cb_ref[:, cols])
        carry_ref[:, cols] = cur[t - SUBLANES:]
        act = _silu(conv)
        if lo < SSD_WIDTH:
            xs_ref[:, cols] = act
        else:
            bc_ref[:, lo - SSD_WIDTH:lo - SSD_WIDTH + CONV_COL_TILE] = act.astype(BF16)

    lane = lax.broadcasted_iota(jnp.int32, (t, RET_QK_WIDTH), 1)
    first_half = (lane % RET_QK_DIM) < (RET_QK_DIM // 2)
    cos = jnp.concatenate([cos_ref[...]] * (RET_QK_WIDTH // LANES), axis=1)
    sin = jnp.concatenate([sin_ref[...]] * (RET_QK_WIDTH // LANES), axis=1)

    def rotary(u):
        swapped = jnp.where(first_half,
                            pltpu.roll(u, RET_QK_WIDTH - RET_QK_DIM // 2, 1),
                            pltpu.roll(u, RET_QK_DIM // 2, 1))
        return u * cos + swapped * sin

    q_ref[...] = rotary(_dot(h, win_ref[:, OFF_Q:OFF_Q + RET_QK_WIDTH]))
    k_ref[...] = rotary(_dot(h, win_ref[:, OFF_K:OFF_K + RET_QK_WIDTH])) * (RET_QK_DIM ** -0.5)
    v_ref[...] = _dot(h, win_ref[:, OFF_V:OFF_V + RET_V_WIDTH]).astype(BF16)
    sg_ref[...] = _silu(_dot(h, win_ref[:, OFF_G:OFF_G + RET_V_WIDTH]))

    row = lax.broadcasted_iota(jnp.int32, (CHUNK, CHUNK), 0)
    col = lax.broadcasted_iota(jnp.int32, (CHUNK, CHUNK), 1)
    causal = row >= col
    tril2 = jnp.where(jnp.concatenate([causal, causal], axis=1), 1.0, 0.0).astype(BF16)
    lane_lo = col < RET_QK_DIM
    row_lo = row < RET_QK_DIM

    def expand_heads(w):
        hi, lo = _split_bf16(w)
        return _dot(jnp.concatenate([hi, lo], axis=1), expand_ref[...])

    def chunk_body(c, carry):
        rows = pl.ds(c * CHUNK, CHUNK)

        dt = dt_ref[rows, :]
        da = dt * a_ref[...]
        da_hi, da_lo = _split_bf16(da)
        acs = _dot(tril2, jnp.concatenate([da_hi, da_lo], axis=0))
        acs_last = acs[CHUNK - 1:CHUNK, :]
        decay_in = expand_heads(dt * jnp.exp(acs_last - acs))
        decay_out = expand_heads(jnp.exp(acs))
        acs_t = acs.T
        dt_t = dt.T
        ys = []
        for g in range(SSD_GROUPS):
            gcols = slice(g * SSD_GROUP_WIDTH, (g + 1) * SSD_GROUP_WIDTH)
            bm = bc_ref[rows, g * SSD_STATE:(g + 1) * SSD_STATE]
            cm = bc_ref[rows, (SSD_GROUPS + g) * SSD_STATE:(SSD_GROUPS + g + 1) * SSD_STATE]
            xs_g = xs_ref[rows, gcols]
            cb = _dot_nt(cm, bm)
            st = st_ref[g]
            y_off = _dot(cm, st.astype(BF16)) * decay_out[:, gcols]
            new = _dot_tn(bm, (xs_g * decay_in[:, gcols]).astype(BF16))
            st_ref[g] = st * decay_out[CHUNK - 1:CHUNK, gcols] + new
            y_diag = []
            for j in range(SSD_GROUP_WIDTH // LANES):
                gs = []
                for k in (g * 8 + 2 * j, g * 8 + 2 * j + 1):
                    seg = acs[:, k:k + 1] - acs_t[k:k + 1, :]
                    lmat = jnp.exp(jnp.where(causal, seg, -jnp.inf))
                    gs.append((cb * lmat * dt_t[k:k + 1, :]).astype(BF16))
                xp = xs_g[:, j * LANES:(j + 1) * LANES]
                rhs = jnp.concatenate([jnp.where(lane_lo, xp, 0.0),
                                       jnp.where(lane_lo, 0.0, xp)], axis=0).astype(BF16)
                y_diag.append(_dot(jnp.concatenate(gs, axis=1), rhs))
            y = jnp.concatenate(y_diag, axis=1) + y_off + xs_g * dskip_ref[:, gcols]
            ys.append(y * _silu(z_ref[rows, gcols]))
        y = jnp.concatenate(ys, axis=1)
        ymix_ref[rows, 0:SSD_WIDTH] = _rmsnorm(y, snw_ref[...]).astype(BF16)

        for j in range(RET_QK_WIDTH // LANES):
            cols = slice(j * LANES, (j + 1) * LANES)
            qp = q_ref[rows, cols]
            kp = k_ref[rows, cols]
            qd = qp * qdec_ref[:, cols]
            kd = (kp * kdec_ref[:, cols]).astype(BF16)
            vp = v_ref[rows, 2 * j * RET_V_DIM:(2 * j + 2) * RET_V_DIM]
            rp = r_ref[cols, :]
            rp_b = rp.astype(BF16)
            q2 = jnp.concatenate([jnp.where(lane_lo, qp, 0.0),
                                  jnp.where(lane_lo, 0.0, qp)], axis=0).astype(BF16)
            sc = _dot_nt(q2, kp.astype(BF16))
            kv = _dot_tn(kd, vp)
            r_ref[cols, :] = rp * cgam_ref[cols, :] + jnp.where(
                row_lo, kv[:, :RET_V_DIM], kv[:, RET_V_DIM:])
            for s in range(2):
                hd = 2 * j + s
                p = (sc[s * CHUNK:(s + 1) * CHUNK] * dmask_ref[hd]).astype(BF16)
                qm = jnp.where(lane_lo, qd, 0.0) if s == 0 else jnp.where(lane_lo, 0.0, qd)
                o = _dot(jnp.concatenate([p, qm.astype(BF16)], axis=1),
                         jnp.concatenate([vp[:, s * RET_V_DIM:(s + 1) * RET_V_DIM], rp_b], axis=0))
                hcols = slice(hd * RET_V_DIM, (hd + 1) * RET_V_DIM)
                on = _rmsnorm(o, rnw_ref[:, hcols]) * sg_ref[rows, hcols]
                ymix_ref[rows, SSD_WIDTH + hd * RET_V_DIM:SSD_WIDTH + (hd + 1) * RET_V_DIM] = (
                    on.astype(BF16))
        return carry

    for c in range(t // CHUNK):
        chunk_body(c, 0)

    y = _dot(ymix_ref[...], wout_ref[...])
    o_ref[0] = x + _rmsnorm(y, pw_ref[...])


def _ffn_kernel(x_ref, nw_ref, wup_ref, cw_ref, cb_ref, wdn_ref, pw_ref, o_ref,
                carry_ref, act_ref):
    t = x_ref.shape[1]

    @pl.when(pl.program_id(1) == 0)
    def _():
        carry_ref[...] = jnp.zeros_like(carry_ref)

    x = x_ref[0]
    h = _rmsnorm(x, nw_ref[...]).astype(BF16)
    for j in range(D_FF // MXU_COLS):
        lo = j * MXU_COLS
        cols = slice(lo, lo + MXU_COLS)
        gate = _dot(h, wup_ref[:, cols])
        val = _dot(h, wup_ref[:, D_FF + lo:D_FF + lo + MXU_COLS])
        d1, d2 = _shifted_rows(carry_ref[:, cols], gate, FFN_CONV - 1)
        cw = cw_ref[:, cols]
        conv = cw[2:3] * gate + cw[1:2] * d1 + cw[0:1] * d2 + cb_ref[:, cols]
        carry_ref[:, cols] = gate[t - SUBLANES:]
        act_ref[:, cols] = (_silu(conv) * val).astype(BF16)
    f = _dot(act_ref[...], wdn_ref[...])
    o_ref[0] = x + _rmsnorm(f, pw_ref[...])


def _resident(shape):
    return pl.BlockSpec(shape, lambda b, i: (0,) * len(shape),
                        pipeline_mode=pl.Buffered(1))


def _compiler_params():
    return pltpu.CompilerParams(dimension_semantics=("arbitrary", "arbitrary"),
                                vmem_limit_bytes=VMEM_LIMIT_BYTES)


def _retention_tables(seq):
    half = RET_QK_DIM // 2
    inv = ROPE_BASE ** (-jnp.arange(0, RET_QK_DIM, 2, dtype=F32) / RET_QK_DIM)
    ang = jnp.arange(seq, dtype=F32)[:, None] * inv[None, :]
    cos, sin = jnp.cos(ang), jnp.sin(ang)
    reps = LANES // RET_QK_DIM
    cos_t = jnp.tile(jnp.concatenate([cos, cos], axis=1), (1, reps))
    sin_t = jnp.tile(jnp.concatenate([-sin, sin], axis=1), (1, reps))
    log_gamma = jnp.log1p(-jnp.exp2(-5.0 - jnp.arange(RET_HEADS, dtype=F32)))
    pos = jnp.arange(CHUNK, dtype=F32)
    rel = pos[:, None] - pos[None, :]
    dmask = jnp.where(rel >= 0, jnp.exp(log_gamma[:, None, None] * jnp.maximum(rel, 0.0)), 0.0)
    k_decay = jnp.exp(log_gamma[:, None] * (CHUNK - 1 - pos)[None, :])
    q_decay = jnp.exp(log_gamma[:, None] * (pos + 1.0)[None, :])
    chunk_gamma = jnp.exp(log_gamma * CHUNK)
    kdec = jnp.repeat(k_decay.T, RET_QK_DIM, axis=1)
    qdec = jnp.repeat(q_decay.T, RET_QK_DIM, axis=1)
    cgam = jnp.broadcast_to(jnp.repeat(chunk_gamma, RET_QK_DIM)[:, None],
                            (RET_QK_WIDTH, RET_V_DIM))
    del half
    return cos_t, sin_t, dmask, qdec, kdec, cgam


def _pad_lanes(v):
    return jnp.pad(v, (0, LANES - v.shape[0])).reshape(1, LANES)


def _mixer(x, pre_w, w_in, conv_w, conv_b, dt_bias, a_log, d_skip, ssd_nw, ret_nw, w_out,
           post_w):
    bsz, seq, d = x.shape
    t = MIX_BLOCK_T
    wz, wxbc, wdt, wq, wk, wv, wg = jnp.split(
        w_in, np.cumsum([SSD_WIDTH, SSD_XBC_WIDTH, SSD_HEADS, RET_QK_WIDTH, RET_QK_WIDTH,
                         RET_V_WIDTH]).tolist(), axis=-1)
    w_packed = jnp.concatenate(
        [wz, wxbc, wq, wk, wv, wg, jnp.pad(wdt, ((0, 0), (0, LANES - SSD_HEADS)))],
        axis=-1).astype(BF16)
    a_neg = _pad_lanes(-jnp.exp(a_log.astype(F32)))
    dskip = jnp.repeat(d_skip.astype(F32), SSD_HEAD_DIM).reshape(1, SSD_WIDTH)
    head_of_col = jnp.arange(SSD_WIDTH) // SSD_HEAD_DIM
    expand = (jnp.arange(LANES)[:, None] == head_of_col[None, :]).astype(BF16)
    expand2 = jnp.concatenate([expand, expand], axis=0)
    cos_t, sin_t, dmask, qdec, kdec, cgam = _retention_tables(seq)

    blk = lambda w: pl.BlockSpec((1, t, w), lambda b, i: (b, i, 0))
    return pl.pallas_call(
        _mixer_kernel,
        grid=(bsz, seq // t),
        in_specs=[
            blk(d),
            _resident((1, d)),
            _resident((d, IN_PACKED)),
            _resident((SSD_CONV, SSD_XBC_WIDTH)),
            _resident((1, SSD_XBC_WIDTH)),
            _resident((1, LANES)),
            _resident((1, LANES)),
            _resident((1, SSD_WIDTH)),
            _resident((1, SSD_WIDTH)),
            _resident((1, RET_V_WIDTH)),
            _resident((MIX_WIDTH, d)),
            _resident((1, d)),
            pl.BlockSpec((t, LANES), lambda b, i: (i, 0)),
            pl.BlockSpec((t, LANES), lambda b, i: (i, 0)),
            _resident((2 * LANES, SSD_WIDTH)),
            _resident((RET_HEADS, CHUNK, CHUNK)),
            _resident((CHUNK, RET_QK_WIDTH)),
            _resident((CHUNK, RET_QK_WIDTH)),
            _resident((RET_QK_WIDTH, RET_V_DIM)),
        ],
        out_specs=blk(d),
        out_shape=jax.ShapeDtypeStruct(x.shape, F32),
        scratch_shapes=[
            pltpu.VMEM((SUBLANES, SSD_XBC_WIDTH), F32),
            pltpu.VMEM((t, SSD_WIDTH), F32),
            pltpu.VMEM((t, SSD_WIDTH), F32),
            pltpu.VMEM((t, SSD_BC_WIDTH), BF16),
            pltpu.VMEM((t, LANES), F32),
            pltpu.VMEM((t, RET_QK_WIDTH), F32),
            pltpu.VMEM((t, RET_QK_WIDTH), F32),
            pltpu.VMEM((t, RET_V_WIDTH), BF16),
            pltpu.VMEM((t, RET_V_WIDTH), F32),
            pltpu.VMEM((t, MIX_WIDTH), BF16),
            pltpu.VMEM((SSD_GROUPS, SSD_STATE, SSD_GROUP_WIDTH), F32),
            pltpu.VMEM((RET_QK_WIDTH, RET_V_DIM), F32),
        ],
        compiler_params=_compiler_params(),
        name="mixer",
    )(x, pre_w.reshape(1, d), w_packed, conv_w, conv_b.reshape(1, -1),
      _pad_lanes(dt_bias.astype(F32)), a_neg, dskip, ssd_nw.reshape(1, -1),
      ret_nw.reshape(1, -1), w_out.astype(BF16), post_w.reshape(1, d), cos_t, sin_t,
      expand2, dmask, qdec, kdec, cgam)


def _ffn(x, pre_w, w_up, conv_w, conv_b, w_down, post_w):
    bsz, seq, d = x.shape
    t = FFN_BLOCK_T
    blk = pl.BlockSpec((1, t, d), lambda b, i: (b, i, 0))
    return pl.pallas_call(
        _ffn_kernel,
        grid=(bsz, seq // t),
        in_specs=[
            blk,
            _resident((1, d)),
            _resident((d, 2 * D_FF)),
            _resident((FFN_CONV, D_FF)),
            _resident((1, D_FF)),
            _resident((D_FF, d)),
            _resident((1, d)),
        ],
        out_specs=blk,
        out_shape=jax.ShapeDtypeStruct(x.shape, F32),
        scratch_shapes=[
            pltpu.VMEM((SUBLANES, D_FF), F32),
            pltpu.VMEM((t, D_FF), BF16),
        ],
        compiler_params=_compiler_params(),
        name="conv_ffn",
    )(x, pre_w.reshape(1, d), w_up.astype(BF16), conv_w, conv_b.reshape(1, D_FF),
      w_down.astype(BF16), post_w.reshape(1, d))


def kernel(x, pre_mix_norm_w, w_in, ssd_conv_w, ssd_conv_b, ssd_dt_bias, ssd_a_log, ssd_d, ssd_norm_w, ret_norm_w, w_out, post_mix_norm_w, pre_ffn_norm_w, w_up, ffn_conv_w, ffn_conv_b, w_down, post_ffn_norm_w):
    for l in range(pre_mix_norm_w.shape[0]):
        x = _mixer(x, pre_mix_norm_w[l], w_in[l], ssd_conv_w[l], ssd_conv_b[l],
                   ssd_dt_bias[l], ssd_a_log[l], ssd_d[l], ssd_norm_w[l], ret_norm_w[l],
                   w_out[l], post_mix_norm_w[l])
        x = _ffn(x, pre_ffn_norm_w[l], w_up[l], ffn_conv_w[l], ffn_conv_b[l], w_down[l],
                 post_ffn_norm_w[l])
    return x
```

```python
import jax
import jax.numpy as jnp
import numpy as np
from jax import lax
from jax.experimental import pallas as pl
from jax.experimental.pallas import tpu as pltpu

D_MODEL = 1024
SSD_HEADS = 16
SSD_HEAD_DIM = 64
SSD_WIDTH = SSD_HEADS * SSD_HEAD_DIM
SSD_GROUPS = 2
SSD_STATE = 128
SSD_CONV = 4
SSD_BC_WIDTH = 2 * SSD_GROUPS * SSD_STATE
SSD_XBC_WIDTH = SSD_WIDTH + SSD_BC_WIDTH
SSD_GROUP_WIDTH = SSD_WIDTH // SSD_GROUPS
RET_HEADS = 8
RET_QK_DIM = 64
RET_V_DIM = 128
RET_QK_WIDTH = RET_HEADS * RET_QK_DIM
RET_V_WIDTH = RET_HEADS * RET_V_DIM
ROPE_BASE = 10000.0
CHUNK = 128
MIX_WIDTH = SSD_WIDTH + RET_V_WIDTH
D_FF = 2816
FFN_CONV = 3
EPS = 1e-6

SUBLANES = 8
LANES = 128
MXU_COLS = 256
MIX_BLOCK_T = 512
FFN_BLOCK_T = 512
CONV_COL_TILE = 512
VMEM_LIMIT_BYTES = 56 * 1024 * 1024

OFF_Z = 0
OFF_XBC = OFF_Z + SSD_WIDTH
OFF_Q = OFF_XBC + SSD_XBC_WIDTH
OFF_K = OFF_Q + RET_QK_WIDTH
OFF_V = OFF_K + RET_QK_WIDTH
OFF_G = OFF_V + RET_V_WIDTH
OFF_DT = OFF_G + RET_V_WIDTH
IN_PACKED = OFF_DT + LANES

F32 = jnp.float32
BF16 = jnp.bfloat16


def _rmsnorm(x, w):
    return x * lax.rsqrt(jnp.mean(x * x, axis=-1, keepdims=True) + EPS) * w


LOG2_E = 1.4426950408889634


def _silu(x):
    return x * (1.0 / (1.0 + jnp.exp2(x * (-LOG2_E))))


def _softplus(x):
    return jnp.maximum(x, 0.0) + jnp.log1p(jnp.exp(-jnp.abs(x)))


def _dot(a, b):
    return jnp.dot(a, b, preferred_element_type=F32)


def _dot_nt(a, b):
    return lax.dot_general(a, b, (((1,), (1,)), ((), ())), preferred_element_type=F32)


def _dot_tn(a, b):
    return lax.dot_general(a, b, (((0,), (0,)), ((), ())), preferred_element_type=F32)


def _split_bf16(x):
    hi = x.astype(BF16)
    lo = (x - hi.astype(F32)).astype(BF16)
    return hi, lo


def _shifted_rows(carry, cur, n_shifts):
    cat = jnp.concatenate([carry, cur], axis=0)
    return [pltpu.roll(cat, s, 0)[SUBLANES:] for s in range(1, n_shifts + 1)]


def _mixer_kernel(x_ref, nw_ref, win_ref, cw_ref, cb_ref, dtb_ref, a_ref, dskip_ref,
                  snw_ref, rnw_ref, wout_ref, pw_ref, cos_ref, sin_ref, expand_ref,
                  dmask_ref, qdec_ref, kdec_ref, cgam_ref, o_ref,
                  carry_ref, sz_ref, xs_ref, xlo_ref, xhi_ref, bc_ref, dt_ref, acs_ref, g_ref,
                  qlo_ref, qhi_ref, qdlo_ref, qdhi_ref, kb_ref, kd_ref, v_ref, sg_ref,
                  ymix_ref, st_ref, r_ref):
    t = x_ref.shape[1]
    n_chunks = t // CHUNK

    @pl.when(pl.program_id(1) == 0)
    def _():
        carry_ref[...] = jnp.zeros_like(carry_ref)
        st_ref[...] = jnp.zeros_like(st_ref)
        r_ref[...] = jnp.zeros_like(r_ref)

    row = lax.broadcasted_iota(jnp.int32, (CHUNK, CHUNK), 0)
    col = lax.broadcasted_iota(jnp.int32, (CHUNK, CHUNK), 1)
    causal = row >= col
    tril2 = jnp.where(jnp.concatenate([causal, causal], axis=1), 1.0, 0.0).astype(BF16)
    row_lo = row < RET_QK_DIM

    def pair_masks(width):
        lane = lax.broadcasted_iota(jnp.int32, (t, width), 1)
        return (lane % LANES) < (LANES // 2)

    x = x_ref[0]
    h = _rmsnorm(x, nw_ref[...]).astype(BF16)
    dt_ref[...] = _softplus(_dot(h, win_ref[:, OFF_DT:OFF_DT + LANES]) + dtb_ref[...])
    lo_c = pair_masks(CONV_COL_TILE)
    for j in range(SSD_XBC_WIDTH // CONV_COL_TILE):
        lo = j * CONV_COL_TILE
        cols = slice(lo, lo + CONV_COL_TILE)
        cur = _dot(h, win_ref[:, OFF_XBC + lo:OFF_XBC + lo + CONV_COL_TILE])
        d1, d2, d3 = _shifted_rows(carry_ref[:, cols], cur, SSD_CONV - 1)
        cw = cw_ref[:, cols]
        conv = (cw[3:4] * cur + cw[2:3] * d1 + cw[1:2] * d2 + cw[0:1] * d3
                + cb_ref[:, cols])
        carry_ref[:, cols] = cur[t - SUBLANES:]
        act = _silu(conv)
        if lo < SSD_WIDTH:
            xs_ref[:, cols] = act
            xlo_ref[:, cols] = jnp.where(lo_c, act, 0.0).astype(BF16)
            xhi_ref[:, cols] = jnp.where(lo_c, 0.0, act).astype(BF16)
        else:
            bc_ref[:, lo - SSD_WIDTH:lo - SSD_WIDTH + CONV_COL_TILE] = act.astype(BF16)

    def ssd_decay_matrices(c):
        rows = pl.ds(c * CHUNK, CHUNK)
        dt = dt_ref[rows, :]
        da = dt * a_ref[...]
        da_hi, da_lo = _split_bf16(da)
        acs = _dot(tril2, jnp.concatenate([da_hi, da_lo], axis=0))
        acs_ref[rows, :] = acs
        acs_t = acs.T
        dt_t = dt.T
        for g in range(SSD_GROUPS):
            bm = bc_ref[rows, g * SSD_STATE:(g + 1) * SSD_STATE]
            cm = bc_ref[rows, (SSD_GROUPS + g) * SSD_STATE:(SSD_GROUPS + g + 1) * SSD_STATE]
            cb = _dot_nt(cm, bm)
            for j in range(SSD_GROUP_WIDTH // LANES):
                for s in range(2):
                    k = g * 8 + 2 * j + s
                    seg = acs[:, k:k + 1] - acs_t[k:k + 1, :]
                    lmat = jnp.exp(jnp.where(causal, seg, -jnp.inf))
                    g_ref[c * (SSD_HEADS // 2) + g * 4 + j, :, s * CHUNK:(s + 1) * CHUNK] = (
                        (cb * lmat * dt_t[k:k + 1, :]).astype(BF16))

    first_half = (lax.broadcasted_iota(jnp.int32, (t, RET_QK_WIDTH), 1) % RET_QK_DIM
                  < RET_QK_DIM // 2)
    cos = jnp.concatenate([cos_ref[...]] * (RET_QK_WIDTH // LANES), axis=1)
    sin = jnp.concatenate([sin_ref[...]] * (RET_QK_WIDTH // LANES), axis=1)
    lo_q = pair_masks(RET_QK_WIDTH)

    def rotary(u):
        swapped = jnp.where(first_half,
                            pltpu.roll(u, RET_QK_WIDTH - RET_QK_DIM // 2, 1),
                            pltpu.roll(u, RET_QK_DIM // 2, 1))
        return u * cos + swapped * sin

    def project_z():
        sz_ref[...] = _silu(_dot(h, win_ref[:, OFF_Z:OFF_Z + SSD_WIDTH]))

    def project_q():
        q = rotary(_dot(h, win_ref[:, OFF_Q:OFF_Q + RET_QK_WIDTH]))
        qd = q * jnp.concatenate([qdec_ref[...]] * n_chunks, axis=0)
        qlo_ref[...] = jnp.where(lo_q, q, 0.0).astype(BF16)
        qhi_ref[...] = jnp.where(lo_q, 0.0, q).astype(BF16)
        qdlo_ref[...] = jnp.where(lo_q, qd, 0.0).astype(BF16)
        qdhi_ref[...] = jnp.where(lo_q, 0.0, qd).astype(BF16)

    def project_k():
        k = rotary(_dot(h, win_ref[:, OFF_K:OFF_K + RET_QK_WIDTH])) * (RET_QK_DIM ** -0.5)
        kb_ref[...] = k.astype(BF16)
        kd_ref[...] = (k * jnp.concatenate([kdec_ref[...]] * n_chunks, axis=0)).astype(BF16)

    def project_v():
        v_ref[...] = _dot(h, win_ref[:, OFF_V:OFF_V + RET_V_WIDTH]).astype(BF16)

    def project_g():
        sg_ref[...] = _silu(_dot(h, win_ref[:, OFF_G:OFF_G + RET_V_WIDTH]))

    def expand_heads(w):
        hi, lo = _split_bf16(w)
        return _dot(jnp.concatenate([hi, lo], axis=1), expand_ref[...])

    def ssd_chunk(c):
        rows = pl.ds(c * CHUNK, CHUNK)
        dt = dt_ref[rows, :]
        acs = acs_ref[rows, :]
        acs_last = acs[CHUNK - 1:CHUNK, :]
        decay_in = expand_heads(dt * jnp.exp(acs_last - acs))
        decay_out = expand_heads(jnp.exp(acs))
        ys = []
        for g in range(SSD_GROUPS):
            gcols = slice(g * SSD_GROUP_WIDTH, (g + 1) * SSD_GROUP_WIDTH)
            bm = bc_ref[rows, g * SSD_STATE:(g + 1) * SSD_STATE]
            cm = bc_ref[rows, (SSD_GROUPS + g) * SSD_STATE:(SSD_GROUPS + g + 1) * SSD_STATE]
            xs_g = xs_ref[rows, gcols]
            st = st_ref[g]
            y_off = _dot(cm, st.astype(BF16)) * decay_out[:, gcols]
            new = _dot_tn(bm, (xs_g * decay_in[:, gcols]).astype(BF16))
            st_ref[g] = st * decay_out[CHUNK - 1:CHUNK, gcols] + new
            y_diag = []
            for j in range(SSD_GROUP_WIDTH // LANES):
                pcols = slice(g * SSD_GROUP_WIDTH + j * LANES,
                              g * SSD_GROUP_WIDTH + (j + 1) * LANES)
                rhs = jnp.concatenate([xlo_ref[rows, pcols], xhi_ref[rows, pcols]], axis=0)
                y_diag.append(_dot(g_ref[c * (SSD_HEADS // 2) + g * 4 + j], rhs))
            y = jnp.concatenate(y_diag, axis=1) + y_off + xs_g * dskip_ref[:, gcols]
            ys.append(y * sz_ref[rows, gcols])
        y = jnp.concatenate(ys, axis=1)
        ymix_ref[rows, 0:SSD_WIDTH] = _rmsnorm(y, snw_ref[...]).astype(BF16)

    def ret_chunk(c):
        rows = pl.ds(c * CHUNK, CHUNK)
        for j in range(RET_QK_WIDTH // LANES):
            cols = slice(j * LANES, (j + 1) * LANES)
            vp = v_ref[rows, 2 * j * RET_V_DIM:(2 * j + 2) * RET_V_DIM]
            rp = r_ref[cols, :]
            rp_b = rp.astype(BF16)
            q2 = jnp.concatenate([qlo_ref[rows, cols], qhi_ref[rows, cols]], axis=0)
            sc = _dot_nt(q2, kb_ref[rows, cols])
            kv = _dot_tn(kd_ref[rows, cols], vp)
            r_ref[cols, :] = rp * cgam_ref[cols, :] + jnp.where(
                row_lo, kv[:, :RET_V_DIM], kv[:, RET_V_DIM:])
            for s in range(2):
                hd = 2 * j + s
                p = (sc[s * CHUNK:(s + 1) * CHUNK] * dmask_ref[hd]).astype(BF16)
                qm = (qdlo_ref if s == 0 else qdhi_ref)[rows, cols]
                o = _dot(jnp.concatenate([p, qm], axis=1),
                         jnp.concatenate([vp[:, s * RET_V_DIM:(s + 1) * RET_V_DIM], rp_b], axis=0))
                hcols = slice(hd * RET_V_DIM, (hd + 1) * RET_V_DIM)
                on = _rmsnorm(o, rnw_ref[:, hcols]) * sg_ref[rows, hcols]
                ymix_ref[rows, SSD_WIDTH + hd * RET_V_DIM:SSD_WIDTH + (hd + 1) * RET_V_DIM] = (
                    on.astype(BF16))

    projections = [project_z, project_q, project_k, project_v, project_g]
    for c in range(n_chunks):
        ssd_decay_matrices(c)
        if c < len(projections):
            projections[c]()
    for p in projections[n_chunks:]:
        p()
    for c in range(n_chunks):
        ssd_chunk(c)
        ret_chunk(c)

    y = _dot(ymix_ref[...], wout_ref[...])
    o_ref[0] = x + _rmsnorm(y, pw_ref[...])


def _ffn_kernel(x_ref, nw_ref, wup_ref, cw_ref, cb_ref, wdn_ref, pw_ref, o_ref,
                carry_ref, act_ref):
    t = x_ref.shape[1]

    @pl.when(pl.program_id(1) == 0)
    def _():
        carry_ref[...] = jnp.zeros_like(carry_ref)

    x = x_ref[0]
    h = _rmsnorm(x, nw_ref[...]).astype(BF16)
    for j in range(D_FF // MXU_COLS):
        lo = j * MXU_COLS
        cols = slice(lo, lo + MXU_COLS)
        gate = _dot(h, wup_ref[:, cols])
        val = _dot(h, wup_ref[:, D_FF + lo:D_FF + lo + MXU_COLS])
        d1, d2 = _shifted_rows(carry_ref[:, cols], gate, FFN_CONV - 1)
        cw = cw_ref[:, cols]
        conv = cw[2:3] * gate + cw[1:2] * d1 + cw[0:1] * d2 + cb_ref[:, cols]
        carry_ref[:, cols] = gate[t - SUBLANES:]
        act_ref[:, cols] = (_silu(conv) * val).astype(BF16)
    f = _dot(act_ref[...], wdn_ref[...])
    o_ref[0] = x + _rmsnorm(f, pw_ref[...])


def _resident(shape):
    return pl.BlockSpec(shape, lambda b, i: (0,) * len(shape),
                        pipeline_mode=pl.Buffered(1))


def _compiler_params():
    return pltpu.CompilerParams(dimension_semantics=("arbitrary", "arbitrary"),
                                vmem_limit_bytes=VMEM_LIMIT_BYTES)


def _retention_tables(seq):
    f64 = np.float64
    inv = ROPE_BASE ** (-np.arange(0, RET_QK_DIM, 2, dtype=f64) / RET_QK_DIM)
    ang = np.arange(seq, dtype=f64)[:, None] * inv[None, :]
    cos, sin = np.cos(ang), np.sin(ang)
    reps = LANES // RET_QK_DIM
    cos_t = np.tile(np.concatenate([cos, cos], axis=1), (1, reps))
    sin_t = np.tile(np.concatenate([-sin, sin], axis=1), (1, reps))
    log_gamma = np.log1p(-np.exp2(-5.0 - np.arange(RET_HEADS, dtype=f64)))
    pos = np.arange(CHUNK, dtype=f64)
    rel = pos[:, None] - pos[None, :]
    dmask = np.where(rel >= 0, np.exp(log_gamma[:, None, None] * np.maximum(rel, 0.0)), 0.0)
    k_decay = np.exp(log_gamma[:, None] * (CHUNK - 1 - pos)[None, :])
    q_decay = np.exp(log_gamma[:, None] * (pos + 1.0)[None, :])
    chunk_gamma = np.exp(log_gamma * CHUNK)
    kdec = np.repeat(k_decay.T, RET_QK_DIM, axis=1)
    qdec = np.repeat(q_decay.T, RET_QK_DIM, axis=1)
    cgam = np.broadcast_to(np.repeat(chunk_gamma, RET_QK_DIM)[:, None],
                           (RET_QK_WIDTH, RET_V_DIM))
    return tuple(jnp.asarray(a, dtype=F32) for a in (cos_t, sin_t, dmask, qdec, kdec, cgam))


def _pad_lanes(v):
    return jnp.pad(v, (0, LANES - v.shape[0])).reshape(1, LANES)


def _mixer(x, pre_w, w_in, conv_w, conv_b, dt_bias, a_log, d_skip, ssd_nw, ret_nw, w_out,
           post_w):
    bsz, seq, d = x.shape
    t = MIX_BLOCK_T
    dt_lo = SSD_WIDTH + SSD_XBC_WIDTH
    w_bf = w_in.astype(BF16)
    w_packed = jnp.concatenate(
        [w_bf[:, :dt_lo], w_bf[:, dt_lo + SSD_HEADS:], w_bf[:, dt_lo:dt_lo + SSD_HEADS],
         jnp.zeros((d, LANES - SSD_HEADS), BF16)], axis=-1)
    a_neg = _pad_lanes(-jnp.exp(a_log.astype(F32)))
    dskip = jnp.repeat(d_skip.astype(F32), SSD_HEAD_DIM).reshape(1, SSD_WIDTH)
    head_of_col = jnp.arange(SSD_WIDTH) // SSD_HEAD_DIM
    expand = (jnp.arange(LANES)[:, None] == head_of_col[None, :]).astype(BF16)
    expand2 = jnp.concatenate([expand, expand], axis=0)
    cos_t, sin_t, dmask, qdec, kdec, cgam = _retention_tables(seq)

    blk = lambda w: pl.BlockSpec((1, t, w), lambda b, i: (b, i, 0))
    return pl.pallas_call(
        _mixer_kernel,
        grid=(bsz, seq // t),
        in_specs=[
            blk(d),
            _resident((1, d)),
            _resident((d, IN_PACKED)),
            _resident((SSD_CONV, SSD_XBC_WIDTH)),
            _resident((1, SSD_XBC_WIDTH)),
            _resident((1, LANES)),
            _resident((1, LANES)),
            _resident((1, SSD_WIDTH)),
            _resident((1, SSD_WIDTH)),
            _resident((1, RET_V_WIDTH)),
            _resident((MIX_WIDTH, d)),
            _resident((1, d)),
            pl.BlockSpec((t, LANES), lambda b, i: (i, 0)),
            pl.BlockSpec((t, LANES), lambda b, i: (i, 0)),
            _resident((2 * LANES, SSD_WIDTH)),
            _resident((RET_HEADS, CHUNK, CHUNK)),
            _resident((CHUNK, RET_QK_WIDTH)),
            _resident((CHUNK, RET_QK_WIDTH)),
            _resident((RET_QK_WIDTH, RET_V_DIM)),
        ],
        out_specs=blk(d),
        out_shape=jax.ShapeDtypeStruct(x.shape, F32),
        scratch_shapes=[
            pltpu.VMEM((SUBLANES, SSD_XBC_WIDTH), F32),
            pltpu.VMEM((t, SSD_WIDTH), F32),
            pltpu.VMEM((t, SSD_WIDTH), F32),
            pltpu.VMEM((t, SSD_WIDTH), BF16),
            pltpu.VMEM((t, SSD_WIDTH), BF16),
            pltpu.VMEM((t, SSD_BC_WIDTH), BF16),
            pltpu.VMEM((t, LANES), F32),
            pltpu.VMEM((t, LANES), F32),
            pltpu.VMEM((t // CHUNK * SSD_HEADS // 2, CHUNK, 2 * CHUNK), BF16),
            pltpu.VMEM((t, RET_QK_WIDTH), BF16),
            pltpu.VMEM((t, RET_QK_WIDTH), BF16),
            pltpu.VMEM((t, RET_QK_WIDTH), BF16),
            pltpu.VMEM((t, RET_QK_WIDTH), BF16),
            pltpu.VMEM((t, RET_QK_WIDTH), BF16),
            pltpu.VMEM((t, RET_QK_WIDTH), BF16),
            pltpu.VMEM((t, RET_V_WIDTH), BF16),
            pltpu.VMEM((t, RET_V_WIDTH), F32),
            pltpu.VMEM((t, MIX_WIDTH), BF16),
            pltpu.VMEM((SSD_GROUPS, SSD_STATE, SSD_GROUP_WIDTH), F32),
            pltpu.VMEM((RET_QK_WIDTH, RET_V_DIM), F32),
        ],
        compiler_params=_compiler_params(),
        name="mixer",
    )(x, pre_w.reshape(1, d), w_packed, conv_w, conv_b.reshape(1, -1),
      _pad_lanes(dt_bias.astype(F32)), a_neg, dskip, ssd_nw.reshape(1, -1),
      ret_nw.reshape(1, -1), w_out.astype(BF16), post_w.reshape(1, d), cos_t, sin_t,
      expand2, dmask, qdec, kdec, cgam)


def _ffn(x, pre_w, w_up, conv_w, conv_b, w_down, post_w):
    bsz, seq, d = x.shape
    t = FFN_BLOCK_T
    blk = pl.BlockSpec((1, t, d), lambda b, i: (b, i, 0))
    return pl.pallas_call(
        _ffn_kernel,
        grid=(bsz, seq // t),
        in_specs=[
            blk,
            _resident((1, d)),
            _resident((d, 2 * D_FF)),
            _resident((FFN_CONV, D_FF)),
            _resident((1, D_FF)),
            _resident((D_FF, d)),
            _resident((1, d)),
        ],
        out_specs=blk,
        out_shape=jax.ShapeDtypeStruct(x.shape, F32),
        scratch_shapes=[
            pltpu.VMEM((SUBLANES, D_FF), F32),
            pltpu.VMEM((t, D_FF), BF16),
        ],
        compiler_params=_compiler_params(),
        name="conv_ffn",
    )(x, pre_w.reshape(1, d), w_up.astype(BF16), conv_w, conv_b.reshape(1, D_FF),
      w_down.astype(BF16), post_w.reshape(1, d))


def kernel(x, pre_mix_norm_w, w_in, ssd_conv_w, ssd_conv_b, ssd_dt_bias, ssd_a_log, ssd_d, ssd_norm_w, ret_norm_w, w_out, post_mix_norm_w, pre_ffn_norm_w, w_up, ffn_conv_w, ffn_conv_b, w_down, post_ffn_norm_w):
    for l in range(pre_mix_norm_w.shape[0]):
        x = _mixer(x, pre_mix_norm_w[l], w_in[l], ssd_conv_w[l], ssd_conv_b[l],
                   ssd_dt_bias[l], ssd_a_log[l], ssd_d[l], ssd_norm_w[l], ret_norm_w[l],
                   w_out[l], post_mix_norm_w[l])
        x = _ffn(x, pre_ffn_norm_w[l], w_up[l], ffn_conv_w[l], ffn_conv_b[l], w_down[l],
                 post_ffn_norm_w[l])
    return x
```

```python
import jax
import jax.numpy as jnp
import numpy as np
from jax import lax
from jax.experimental import pallas as pl
from jax.experimental.pallas import tpu as pltpu

D_MODEL = 1024
SSD_HEADS = 16
SSD_HEAD_DIM = 64
SSD_WIDTH = SSD_HEADS * SSD_HEAD_DIM
SSD_GROUPS = 2
SSD_STATE = 128
SSD_CONV = 4
SSD_BC_WIDTH = 2 * SSD_GROUPS * SSD_STATE
SSD_XBC_WIDTH = SSD_WIDTH + SSD_BC_WIDTH
SSD_GROUP_WIDTH = SSD_WIDTH // SSD_GROUPS
RET_HEADS = 8
RET_QK_DIM = 64
RET_V_DIM = 128
RET_QK_WIDTH = RET_HEADS * RET_QK_DIM
RET_V_WIDTH = RET_HEADS * RET_V_DIM
ROPE_BASE = 10000.0
CHUNK = 128
MIX_WIDTH = SSD_WIDTH + RET_V_WIDTH
D_FF = 2816
FFN_CONV = 3
EPS = 1e-6

SUBLANES = 8
LANES = 128
MXU_COLS = 256
MIX_BLOCK_T = 512
FFN_BLOCK_T = 512
CONV_COL_TILE = 512
VMEM_LIMIT_BYTES = 56 * 1024 * 1024

OFF_Z = 0
OFF_XBC = OFF_Z + SSD_WIDTH
OFF_DT = OFF_XBC + SSD_XBC_WIDTH
IN_A_WIDTH = OFF_DT + LANES
IN_B_START = OFF_DT + SSD_HEADS
OFF_Q = 0
OFF_K = OFF_Q + RET_QK_WIDTH
OFF_V = OFF_K + RET_QK_WIDTH
OFF_G = OFF_V + RET_V_WIDTH
IN_B_WIDTH = OFF_G + RET_V_WIDTH

F32 = jnp.float32
BF16 = jnp.bfloat16


def _rmsnorm(x, w):
    return x * lax.rsqrt(jnp.mean(x * x, axis=-1, keepdims=True) + EPS) * w


LOG2_E = 1.4426950408889634


def _silu(x):
    return x * (1.0 / (1.0 + jnp.exp2(x * (-LOG2_E))))


def _softplus(x):
    return jnp.maximum(x, 0.0) + jnp.log1p(jnp.exp(-jnp.abs(x)))


def _dot(a, b):
    return jnp.dot(a, b, preferred_element_type=F32)


def _dot_nt(a, b):
    return lax.dot_general(a, b, (((1,), (1,)), ((), ())), preferred_element_type=F32)


def _dot_tn(a, b):
    return lax.dot_general(a, b, (((0,), (0,)), ((), ())), preferred_element_type=F32)


def _split_bf16(x):
    hi = x.astype(BF16)
    lo = (x - hi.astype(F32)).astype(BF16)
    return hi, lo


def _shifted_rows(carry, cur, n_shifts):
    cat = jnp.concatenate([carry, cur], axis=0)
    return [pltpu.roll(cat, s, 0)[SUBLANES:] for s in range(1, n_shifts + 1)]


def _mixer_kernel(x_ref, nw_ref, wina_ref, winb_ref, cw_ref, cb_ref, dtb_ref, a_ref, dskip_ref,
                  snw_ref, rnw_ref, wout_ref, pw_ref, cos_ref, sin_ref, expand_ref,
                  dmask_ref, qdec_ref, kdec_ref, cgam_ref, o_ref,
                  carry_ref, sz_ref, xs_ref, xlo_ref, xhi_ref, bc_ref, dt_ref, acs_ref, g_ref,
                  qlo_ref, qhi_ref, qdlo_ref, qdhi_ref, kb_ref, kd_ref, v_ref, sg_ref,
                  ymix_ref, st_ref, r_ref):
    t = x_ref.shape[1]
    n_chunks = t // CHUNK

    @pl.when(pl.program_id(1) == 0)
    def _():
        carry_ref[...] = jnp.zeros_like(carry_ref)
        st_ref[...] = jnp.zeros_like(st_ref)
        r_ref[...] = jnp.zeros_like(r_ref)

    row = lax.broadcasted_iota(jnp.int32, (CHUNK, CHUNK), 0)
    col = lax.broadcasted_iota(jnp.int32, (CHUNK, CHUNK), 1)
    causal = row >= col
    tril2 = jnp.where(jnp.concatenate([causal, causal], axis=1), 1.0, 0.0).astype(BF16)
    row_lo = row < RET_QK_DIM

    def pair_masks(width):
        lane = lax.broadcasted_iota(jnp.int32, (t, width), 1)
        return (lane % LANES) < (LANES // 2)

    x = x_ref[0]
    h = _rmsnorm(x, nw_ref[...]).astype(BF16)
    dt_ref[...] = _softplus(_dot(h, wina_ref[:, OFF_DT:OFF_DT + LANES]) + dtb_ref[...])
    lo_c = pair_masks(CONV_COL_TILE)
    for j in range(SSD_XBC_WIDTH // CONV_COL_TILE):
        lo = j * CONV_COL_TILE
        cols = slice(lo, lo + CONV_COL_TILE)
        cur = _dot(h, wina_ref[:, OFF_XBC + lo:OFF_XBC + lo + CONV_COL_TILE])
        d1, d2, d3 = _shifted_rows(carry_ref[:, cols], cur, SSD_CONV - 1)
        cw = cw_ref[:, cols]
        conv = (cw[3:4] * cur + cw[2:3] * d1 + cw[1:2] * d2 + cw[0:1] * d3
                + cb_ref[:, cols])
        carry_ref[:, cols] = cur[t - SUBLANES:]
        act = _silu(conv)
        if lo < SSD_WIDTH:
            xs_ref[:, cols] = act
            xlo_ref[:, cols] = jnp.where(lo_c, act, 0.0).astype(BF16)
            xhi_ref[:, cols] = jnp.where(lo_c, 0.0, act).astype(BF16)
        else:
            bc_ref[:, lo - SSD_WIDTH:lo - SSD_WIDTH + CONV_COL_TILE] = act.astype(BF16)

    def expand_heads(w):
        hi, lo = _split_bf16(w)
        return _dot(jnp.concatenate([hi, lo], axis=1), expand_ref[...])

    def ssd_decay_matrices(c):
        rows = pl.ds(c * CHUNK, CHUNK)
        dt = dt_ref[rows, :]
        da = dt * a_ref[...]
        da_hi, da_lo = _split_bf16(da)
        acs = _dot(tril2, jnp.concatenate([da_hi, da_lo], axis=0))
        acs_ref[rows, :] = acs
        acs_t = acs.T
        dt_t = dt.T
        for g in range(SSD_GROUPS):
            bm = bc_ref[rows, g * SSD_STATE:(g + 1) * SSD_STATE]
            cm = bc_ref[rows, (SSD_GROUPS + g) * SSD_STATE:(SSD_GROUPS + g + 1) * SSD_STATE]
            cb = _dot_nt(cm, bm)
            for j in range(SSD_GROUP_WIDTH // LANES):
                for s in range(2):
                    k = g * 8 + 2 * j + s
                    seg = acs[:, k:k + 1] - acs_t[k:k + 1, :]
                    lmat = jnp.exp(jnp.where(causal, seg, -jnp.inf))
                    g_ref[c * (SSD_HEADS // 2) + g * 4 + j, :, s * CHUNK:(s + 1) * CHUNK] = (
                        (cb * lmat * dt_t[k:k + 1, :]).astype(BF16))

    first_half = (lax.broadcasted_iota(jnp.int32, (t, RET_QK_WIDTH), 1) % RET_QK_DIM
                  < RET_QK_DIM // 2)
    cos = jnp.concatenate([cos_ref[...]] * (RET_QK_WIDTH // LANES), axis=1)
    sin = jnp.concatenate([sin_ref[...]] * (RET_QK_WIDTH // LANES), axis=1)
    lo_q = pair_masks(RET_QK_WIDTH)

    def rotary(u):
        swapped = jnp.where(first_half,
                            pltpu.roll(u, RET_QK_WIDTH - RET_QK_DIM // 2, 1),
                            pltpu.roll(u, RET_QK_DIM // 2, 1))
        return u * cos + swapped * sin

    def project_z():
        sz_ref[...] = _silu(_dot(h, wina_ref[:, OFF_Z:OFF_Z + SSD_WIDTH]))

    def project_q():
        q = rotary(_dot(h, winb_ref[:, OFF_Q:OFF_Q + RET_QK_WIDTH]))
        qd = q * jnp.concatenate([qdec_ref[...]] * n_chunks, axis=0)
        qlo_ref[...] = jnp.where(lo_q, q, 0.0).astype(BF16)
        qhi_ref[...] = jnp.where(lo_q, 0.0, q).astype(BF16)
        qdlo_ref[...] = jnp.where(lo_q, qd, 0.0).astype(BF16)
        qdhi_ref[...] = jnp.where(lo_q, 0.0, qd).astype(BF16)

    def project_k():
        k = rotary(_dot(h, winb_ref[:, OFF_K:OFF_K + RET_QK_WIDTH])) * (RET_QK_DIM ** -0.5)
        kb_ref[...] = k.astype(BF16)
        kd_ref[...] = (k * jnp.concatenate([kdec_ref[...]] * n_chunks, axis=0)).astype(BF16)

    def project_v():
        v_ref[...] = _dot(h, winb_ref[:, OFF_V:OFF_V + RET_V_WIDTH]).astype(BF16)

    def project_g():
        sg_ref[...] = _silu(_dot(h, winb_ref[:, OFF_G:OFF_G + RET_V_WIDTH]))

    def ssd_chunk(c):
        rows = pl.ds(c * CHUNK, CHUNK)
        dt = dt_ref[rows, :]
        acs = acs_ref[rows, :]
        acs_last = acs[CHUNK - 1:CHUNK, :]
        decay_in = expand_heads(dt * jnp.exp(acs_last - acs))
        decay_out = expand_heads(jnp.exp(acs))
        ys = []
        for g in range(SSD_GROUPS):
            gcols = slice(g * SSD_GROUP_WIDTH, (g + 1) * SSD_GROUP_WIDTH)
            bm = bc_ref[rows, g * SSD_STATE:(g + 1) * SSD_STATE]
            cm = bc_ref[rows, (SSD_GROUPS + g) * SSD_STATE:(SSD_GROUPS + g + 1) * SSD_STATE]
            xs_g = xs_ref[rows, gcols]
            st = st_ref[g]
            y_off = _dot(cm, st.astype(BF16)) * decay_out[:, gcols]
            new = _dot_tn(bm, (xs_g * decay_in[:, gcols]).astype(BF16))
            st_ref[g] = st * decay_out[CHUNK - 1:CHUNK, gcols] + new
            y_diag = []
            for j in range(SSD_GROUP_WIDTH // LANES):
                pcols = slice(g * SSD_GROUP_WIDTH + j * LANES,
                              g * SSD_GROUP_WIDTH + (j + 1) * LANES)
                rhs = jnp.concatenate([xlo_ref[rows, pcols], xhi_ref[rows, pcols]], axis=0)
                y_diag.append(_dot(g_ref[c * (SSD_HEADS // 2) + g * 4 + j], rhs))
            y = jnp.concatenate(y_diag, axis=1) + y_off + xs_g * dskip_ref[:, gcols]
            ys.append(y * sz_ref[rows, gcols])
        y = jnp.concatenate(ys, axis=1)
        ymix_ref[rows, 0:SSD_WIDTH] = _rmsnorm(y, snw_ref[...]).astype(BF16)

    def ret_chunk(c):
        rows = pl.ds(c * CHUNK, CHUNK)
        for j in range(RET_QK_WIDTH // LANES):
            cols = slice(j * LANES, (j + 1) * LANES)
            vp = v_ref[rows, 2 * j * RET_V_DIM:(2 * j + 2) * RET_V_DIM]
            rp = r_ref[cols, :]
            rp_b = rp.astype(BF16)
            q2 = jnp.concatenate([qlo_ref[rows, cols], qhi_ref[rows, cols]], axis=0)
            sc = _dot_nt(q2, kb_ref[rows, cols])
            kv = _dot_tn(kd_ref[rows, cols], vp)
            r_ref[cols, :] = rp * cgam_ref[cols, :] + jnp.where(
                row_lo, kv[:, :RET_V_DIM], kv[:, RET_V_DIM:])
            for s in range(2):
                hd = 2 * j + s
                p = (sc[s * CHUNK:(s + 1) * CHUNK] * dmask_ref[hd]).astype(BF16)
                qm = (qdlo_ref if s == 0 else qdhi_ref)[rows, cols]
                o = _dot(jnp.concatenate([p, qm], axis=1),
                         jnp.concatenate([vp[:, s * RET_V_DIM:(s + 1) * RET_V_DIM], rp_b], axis=0))
                hcols = slice(hd * RET_V_DIM, (hd + 1) * RET_V_DIM)
                on = _rmsnorm(o, rnw_ref[:, hcols]) * sg_ref[rows, hcols]
                ymix_ref[rows, SSD_WIDTH + hd * RET_V_DIM:SSD_WIDTH + (hd + 1) * RET_V_DIM] = (
                    on.astype(BF16))

    projections = [project_z, project_q, project_k, project_v, project_g]
    for c in range(n_chunks):
        ssd_decay_matrices(c)
        if c < len(projections):
            projections[c]()
    for p in projections[n_chunks:]:
        p()
    for c in range(n_chunks):
        ssd_chunk(c)
        ret_chunk(c)

    y = _dot(ymix_ref[...], wout_ref[...])
    o_ref[0] = x + _rmsnorm(y, pw_ref[...])


def _ffn_kernel(x_ref, nw_ref, wup_ref, cw_ref, cb_ref, wdn_ref, pw_ref, o_ref,
                carry_ref, act_ref):
    t = x_ref.shape[1]

    @pl.when(pl.program_id(1) == 0)
    def _():
        carry_ref[...] = jnp.zeros_like(carry_ref)

    x = x_ref[0]
    h = _rmsnorm(x, nw_ref[...]).astype(BF16)
    for j in range(D_FF // MXU_COLS):
        lo = j * MXU_COLS
        cols = slice(lo, lo + MXU_COLS)
        gate = _dot(h, wup_ref[:, cols])
        val = _dot(h, wup_ref[:, D_FF + lo:D_FF + lo + MXU_COLS])
        d1, d2 = _shifted_rows(carry_ref[:, cols], gate, FFN_CONV - 1)
        cw = cw_ref[:, cols]
        conv = cw[2:3] * gate + cw[1:2] * d1 + cw[0:1] * d2 + cb_ref[:, cols]
        carry_ref[:, cols] = gate[t - SUBLANES:]
        act_ref[:, cols] = (_silu(conv) * val).astype(BF16)
    f = _dot(act_ref[...], wdn_ref[...])
    o_ref[0] = x + _rmsnorm(f, pw_ref[...])


def _resident(shape):
    return pl.BlockSpec(shape, lambda b, i: (0,) * len(shape),
                        pipeline_mode=pl.Buffered(1))


def _compiler_params():
    return pltpu.CompilerParams(dimension_semantics=("arbitrary", "arbitrary"),
                                vmem_limit_bytes=VMEM_LIMIT_BYTES)


def _retention_tables(seq):
    f64 = np.float64
    inv = ROPE_BASE ** (-np.arange(0, RET_QK_DIM, 2, dtype=f64) / RET_QK_DIM)
    ang = np.arange(seq, dtype=f64)[:, None] * inv[None, :]
    cos, sin = np.cos(ang), np.sin(ang)
    reps = LANES // RET_QK_DIM
    cos_t = np.tile(np.concatenate([cos, cos], axis=1), (1, reps))
    sin_t = np.tile(np.concatenate([-sin, sin], axis=1), (1, reps))
    log_gamma = np.log1p(-np.exp2(-5.0 - np.arange(RET_HEADS, dtype=f64)))
    pos = np.arange(CHUNK, dtype=f64)
    rel = pos[:, None] - pos[None, :]
    dmask = np.where(rel >= 0, np.exp(log_gamma[:, None, None] * np.maximum(rel, 0.0)), 0.0)
    k_decay = np.exp(log_gamma[:, None] * (CHUNK - 1 - pos)[None, :])
    q_decay = np.exp(log_gamma[:, None] * (pos + 1.0)[None, :])
    chunk_gamma = np.exp(log_gamma * CHUNK)
    kdec = np.repeat(k_decay.T, RET_QK_DIM, axis=1)
    qdec = np.repeat(q_decay.T, RET_QK_DIM, axis=1)
    cgam = np.broadcast_to(np.repeat(chunk_gamma, RET_QK_DIM)[:, None],
                           (RET_QK_WIDTH, RET_V_DIM))
    return tuple(jnp.asarray(a, dtype=F32) for a in (cos_t, sin_t, dmask, qdec, kdec, cgam))


def _pad_lanes(v):
    return jnp.pad(v, (0, LANES - v.shape[0])).reshape(1, LANES)


def _mixer(x, pre_w, w_in, conv_w, conv_b, dt_bias, a_log, d_skip, ssd_nw, ret_nw, w_out,
           post_w):
    bsz, seq, d = x.shape
    t = MIX_BLOCK_T
    w_a = w_in[:, :IN_A_WIDTH].astype(BF16)
    w_b = w_in[:, IN_B_START:].astype(BF16)
    a_neg = _pad_lanes(-jnp.exp(a_log.astype(F32)))
    dskip = jnp.repeat(d_skip.astype(F32), SSD_HEAD_DIM).reshape(1, SSD_WIDTH)
    head_of_col = jnp.arange(SSD_WIDTH) // SSD_HEAD_DIM
    expand = (jnp.arange(LANES)[:, None] == head_of_col[None, :]).astype(BF16)
    expand2 = jnp.concatenate([expand, expand], axis=0)
    cos_t, sin_t, dmask, qdec, kdec, cgam = _retention_tables(seq)

    blk = lambda w: pl.BlockSpec((1, t, w), lambda b, i: (b, i, 0))
    return pl.pallas_call(
        _mixer_kernel,
        grid=(bsz, seq // t),
        in_specs=[
            blk(d),
            _resident((1, d)),
            _resident((d, IN_A_WIDTH)),
            _resident((d, IN_B_WIDTH)),
            _resident((SSD_CONV, SSD_XBC_WIDTH)),
            _resident((1, SSD_XBC_WIDTH)),
            _resident((1, LANES)),
            _resident((1, LANES)),
            _resident((1, SSD_WIDTH)),
            _resident((1, SSD_WIDTH)),
            _resident((1, RET_V_WIDTH)),
            _resident((MIX_WIDTH, d)),
            _resident((1, d)),
            pl.BlockSpec((t, LANES), lambda b, i: (i, 0)),
            pl.BlockSpec((t, LANES), lambda b, i: (i, 0)),
            _resident((2 * LANES, SSD_WIDTH)),
            _resident((RET_HEADS, CHUNK, CHUNK)),
            _resident((CHUNK, RET_QK_WIDTH)),
            _resident((CHUNK, RET_QK_WIDTH)),
            _resident((RET_QK_WIDTH, RET_V_DIM)),
        ],
        out_specs=blk(d),
        out_shape=jax.ShapeDtypeStruct(x.shape, F32),
        scratch_shapes=[
            pltpu.VMEM((SUBLANES, SSD_XBC_WIDTH), F32),
            pltpu.VMEM((t, SSD_WIDTH), F32),
            pltpu.VMEM((t, SSD_WIDTH), F32),
            pltpu.VMEM((t, SSD_WIDTH), BF16),
            pltpu.VMEM((t, SSD_WIDTH), BF16),
            pltpu.VMEM((t, SSD_BC_WIDTH), BF16),
            pltpu.VMEM((t, LANES), F32),
            pltpu.VMEM((t, LANES), F32),
            pltpu.VMEM((t // CHUNK * SSD_HEADS // 2, CHUNK, 2 * CHUNK), BF16),
            pltpu.VMEM((t, RET_QK_WIDTH), BF16),
            pltpu.VMEM((t, RET_QK_WIDTH), BF16),
            pltpu.VMEM((t, RET_QK_WIDTH), BF16),
            pltpu.VMEM((t, RET_QK_WIDTH), BF16),
            pltpu.VMEM((t, RET_QK_WIDTH), BF16),
            pltpu.VMEM((t, RET_QK_WIDTH), BF16),
            pltpu.VMEM((t, RET_V_WIDTH), BF16),
            pltpu.VMEM((t, RET_V_WIDTH), F32),
            pltpu.VMEM((t, MIX_WIDTH), BF16),
            pltpu.VMEM((SSD_GROUPS, SSD_STATE, SSD_GROUP_WIDTH), F32),
            pltpu.VMEM((RET_QK_WIDTH, RET_V_DIM), F32),
        ],
        compiler_params=_compiler_params(),
        name="mixer",
    )(x, pre_w.reshape(1, d), w_a, w_b, conv_w, conv_b.reshape(1, -1),
      _pad_lanes(dt_bias.astype(F32)), a_neg, dskip, ssd_nw.reshape(1, -1),
      ret_nw.reshape(1, -1), w_out.astype(BF16), post_w.reshape(1, d), cos_t, sin_t,
      expand2, dmask, qdec, kdec, cgam)


def _ffn(x, pre_w, w_up, conv_w, conv_b, w_down, post_w):
    bsz, seq, d = x.shape
    t = FFN_BLOCK_T
    blk = pl.BlockSpec((1, t, d), lambda b, i: (b, i, 0))
    return pl.pallas_call(
        _ffn_kernel,
        grid=(bsz, seq // t),
        in_specs=[
            blk,
            _resident((1, d)),
            _resident((d, 2 * D_FF)),
            _resident((FFN_CONV, D_FF)),
            _resident((1, D_FF)),
            _resident((D_FF, d)),
            _resident((1, d)),
        ],
        out_specs=blk,
        out_shape=jax.ShapeDtypeStruct(x.shape, F32),
        scratch_shapes=[
            pltpu.VMEM((SUBLANES, D_FF), F32),
            pltpu.VMEM((t, D_FF), BF16),
        ],
        compiler_params=_compiler_params(),
        name="conv_ffn",
    )(x, pre_w.reshape(1, d), w_up.astype(BF16), conv_w, conv_b.reshape(1, D_FF),
      w_down.astype(BF16), post_w.reshape(1, d))


def kernel(x, pre_mix_norm_w, w_in, ssd_conv_w, ssd_conv_b, ssd_dt_bias, ssd_a_log, ssd_d, ssd_norm_w, ret_norm_w, w_out, post_mix_norm_w, pre_ffn_norm_w, w_up, ffn_conv_w, ffn_conv_b, w_down, post_ffn_norm_w):
    for l in range(pre_mix_norm_w.shape[0]):
        x = _mixer(x, pre_mix_norm_w[l], w_in[l], ssd_conv_w[l], ssd_conv_b[l],
                   ssd_dt_bias[l], ssd_a_log[l], ssd_d[l], ssd_norm_w[l], ret_norm_w[l],
                   w_out[l], post_mix_norm_w[l])
        x = _ffn(x, pre_ffn_norm_w[l], w_up[l], ffn_conv_w[l], ffn_conv_b[l], w_down[l],
                 post_ffn_norm_w[l])
    return x
```

```python
import jax
import jax.numpy as jnp
import numpy as np
from jax import lax
from jax.experimental import pallas as pl
from jax.experimental.pallas import tpu as pltpu

D_MODEL = 1024
SSD_HEADS = 16
SSD_HEAD_DIM = 64
SSD_WIDTH = SSD_HEADS * SSD_HEAD_DIM
SSD_GROUPS = 2
SSD_STATE = 128
SSD_CONV = 4
SSD_BC_WIDTH = 2 * SSD_GROUPS * SSD_STATE
SSD_XBC_WIDTH = SSD_WIDTH + SSD_BC_WIDTH
SSD_GROUP_WIDTH = SSD_WIDTH // SSD_GROUPS
RET_HEADS = 8
RET_QK_DIM = 64
RET_V_DIM = 128
RET_QK_WIDTH = RET_HEADS * RET_QK_DIM
RET_V_WIDTH = RET_HEADS * RET_V_DIM
ROPE_BASE = 10000.0
CHUNK = 128
MIX_WIDTH = SSD_WIDTH + RET_V_WIDTH
D_FF = 2816
FFN_CONV = 3
EPS = 1e-6

SUBLANES = 8
LANES = 128
MXU_COLS = 256
MIX_BLOCK_T = 512
FFN_BLOCK_T = 512
CONV_COL_TILE = 512
VMEM_LIMIT_BYTES = 56 * 1024 * 1024

OFF_Z = 0
OFF_XBC = OFF_Z + SSD_WIDTH
OFF_DT = OFF_XBC + SSD_XBC_WIDTH
IN_A_WIDTH = OFF_DT + LANES
IN_B_START = OFF_DT + SSD_HEADS
OFF_Q = 0
OFF_K = OFF_Q + RET_QK_WIDTH
OFF_V = OFF_K + RET_QK_WIDTH
OFF_G = OFF_V + RET_V_WIDTH
IN_B_WIDTH = OFF_G + RET_V_WIDTH

F32 = jnp.float32
BF16 = jnp.bfloat16


def _rmsnorm(x, w):
    return x * lax.rsqrt(jnp.mean(x * x, axis=-1, keepdims=True) + EPS) * w


LOG2_E = 1.4426950408889634


def _silu(x):
    return x * (1.0 / (1.0 + jnp.exp2(x * (-LOG2_E))))


def _softplus(x):
    return jnp.maximum(x, 0.0) + jnp.log1p(jnp.exp(-jnp.abs(x)))


def _dot(a, b):
    return jnp.dot(a, b, preferred_element_type=F32)


def _split_bf16(x):
    hi = x.astype(BF16)
    lo = (x - hi.astype(F32)).astype(BF16)
    return hi, lo


def _shifted_rows(carry, cur, n_shifts):
    cat = jnp.concatenate([carry, cur], axis=0)
    return [pltpu.roll(cat, s, 0)[SUBLANES:] for s in range(1, n_shifts + 1)]


def _mixer_kernel(x_ref, nw_ref, wina_ref, winb_ref, cw_ref, cb_ref, dtb_ref, a_ref, dskip_ref,
                  snw_ref, rnw_ref, wout_ref, pw_ref, cos_ref, sin_ref, expand_ref,
                  dmask_ref, qdec_ref, kdect_ref, cgam_ref, o_ref,
                  carry_ref, sz_ref, xs_ref, xlo_ref, xhi_ref, bmt_ref, cm_ref, dt_ref, acs_ref,
                  g_ref, qlo_ref, qhi_ref, qdlo_ref, qdhi_ref, kbt_ref, kdt_ref, v_ref, sg_ref,
                  ymix_ref, st_ref, r_ref):
    t = x_ref.shape[1]
    n_chunks = t // CHUNK

    @pl.when(pl.program_id(1) == 0)
    def _():
        carry_ref[...] = jnp.zeros_like(carry_ref)
        st_ref[...] = jnp.zeros_like(st_ref)
        r_ref[...] = jnp.zeros_like(r_ref)

    row = lax.broadcasted_iota(jnp.int32, (CHUNK, CHUNK), 0)
    col = lax.broadcasted_iota(jnp.int32, (CHUNK, CHUNK), 1)
    causal = row >= col
    tril2 = jnp.where(jnp.concatenate([causal, causal], axis=1), 1.0, 0.0).astype(BF16)
    row_lo = row < RET_QK_DIM

    def pair_masks(width):
        lane = lax.broadcasted_iota(jnp.int32, (t, width), 1)
        return (lane % LANES) < (LANES // 2)

    x = x_ref[0]
    h = _rmsnorm(x, nw_ref[...]).astype(BF16)
    dt_ref[...] = _softplus(_dot(h, wina_ref[:, OFF_DT:OFF_DT + LANES]) + dtb_ref[...])
    lo_c = pair_masks(CONV_COL_TILE)
    for j in range(SSD_XBC_WIDTH // CONV_COL_TILE):
        lo = j * CONV_COL_TILE
        cols = slice(lo, lo + CONV_COL_TILE)
        cur = _dot(h, wina_ref[:, OFF_XBC + lo:OFF_XBC + lo + CONV_COL_TILE])
        d1, d2, d3 = _shifted_rows(carry_ref[:, cols], cur, SSD_CONV - 1)
        cw = cw_ref[:, cols]
        conv = (cw[3:4] * cur + cw[2:3] * d1 + cw[1:2] * d2 + cw[0:1] * d3
                + cb_ref[:, cols])
        carry_ref[:, cols] = cur[t - SUBLANES:]
        act = _silu(conv)
        if lo < SSD_WIDTH:
            xs_ref[:, cols] = act
            xlo_ref[:, cols] = jnp.where(lo_c, act, 0.0).astype(BF16)
            xhi_ref[:, cols] = jnp.where(lo_c, 0.0, act).astype(BF16)
        else:
            bmt_ref[...] = act[:, :SSD_GROUPS * SSD_STATE].T.astype(BF16)
            cm_ref[...] = act[:, SSD_GROUPS * SSD_STATE:].astype(BF16)

    def expand_heads(w):
        hi, lo = _split_bf16(w)
        return _dot(jnp.concatenate([hi, lo], axis=1), expand_ref[...])

    def ssd_decay_matrices(c):
        rows = pl.ds(c * CHUNK, CHUNK)
        dt = dt_ref[rows, :]
        da = dt * a_ref[...]
        da_hi, da_lo = _split_bf16(da)
        acs = _dot(tril2, jnp.concatenate([da_hi, da_lo], axis=0))
        acs_ref[rows, :] = acs
        acs_t = acs.T
        dt_t = dt.T
        for g in range(SSD_GROUPS):
            bm_t = bmt_ref[g * SSD_STATE:(g + 1) * SSD_STATE, rows]
            cm = cm_ref[rows, g * SSD_STATE:(g + 1) * SSD_STATE]
            cb = _dot(cm, bm_t)
            for j in range(SSD_GROUP_WIDTH // LANES):
                for s in range(2):
                    k = g * 8 + 2 * j + s
                    seg = acs[:, k:k + 1] - acs_t[k:k + 1, :]
                    lmat = jnp.exp(jnp.where(causal, seg, -jnp.inf))
                    g_ref[c * (SSD_HEADS // 2) + g * 4 + j, :, s * CHUNK:(s + 1) * CHUNK] = (
                        (cb * lmat * dt_t[k:k + 1, :]).astype(BF16))

    first_half = (lax.broadcasted_iota(jnp.int32, (t, RET_QK_WIDTH), 1) % RET_QK_DIM
                  < RET_QK_DIM // 2)
    cos = jnp.concatenate([cos_ref[...]] * (RET_QK_WIDTH // LANES), axis=1)
    sin = jnp.concatenate([sin_ref[...]] * (RET_QK_WIDTH // LANES), axis=1)
    lo_q = pair_masks(RET_QK_WIDTH)

    def rotary(u):
        swapped = jnp.where(first_half,
                            pltpu.roll(u, RET_QK_WIDTH - RET_QK_DIM // 2, 1),
                            pltpu.roll(u, RET_QK_DIM // 2, 1))
        return u * cos + swapped * sin

    def project_z():
        sz_ref[...] = _silu(_dot(h, wina_ref[:, OFF_Z:OFF_Z + SSD_WIDTH]))

    def project_q():
        q = rotary(_dot(h, winb_ref[:, OFF_Q:OFF_Q + RET_QK_WIDTH]))
        qd = q * jnp.concatenate([qdec_ref[...]] * n_chunks, axis=0)
        qlo_ref[...] = jnp.where(lo_q, q, 0.0).astype(BF16)
        qhi_ref[...] = jnp.where(lo_q, 0.0, q).astype(BF16)
        qdlo_ref[...] = jnp.where(lo_q, qd, 0.0).astype(BF16)
        qdhi_ref[...] = jnp.where(lo_q, 0.0, qd).astype(BF16)

    def project_k():
        k = rotary(_dot(h, winb_ref[:, OFF_K:OFF_K + RET_QK_WIDTH])) * (RET_QK_DIM ** -0.5)
        k_t = k.T
        kbt_ref[...] = k_t.astype(BF16)
        kdt_ref[...] = (k_t * jnp.concatenate([kdect_ref[...]] * n_chunks, axis=1)).astype(BF16)

    def project_v():
        v_ref[...] = _dot(h, winb_ref[:, OFF_V:OFF_V + RET_V_WIDTH]).astype(BF16)

    def project_g():
        sg_ref[...] = _silu(_dot(h, winb_ref[:, OFF_G:OFF_G + RET_V_WIDTH]))

    def ssd_chunk(c):
        rows = pl.ds(c * CHUNK, CHUNK)
        dt = dt_ref[rows, :]
        acs = acs_ref[rows, :]
        acs_last = acs[CHUNK - 1:CHUNK, :]
        decay_in = expand_heads(dt * jnp.exp(acs_last - acs))
        decay_out = expand_heads(jnp.exp(acs))
        ys = []
        for g in range(SSD_GROUPS):
            gcols = slice(g * SSD_GROUP_WIDTH, (g + 1) * SSD_GROUP_WIDTH)
            bm_t = bmt_ref[g * SSD_STATE:(g + 1) * SSD_STATE, rows]
            cm = cm_ref[rows, g * SSD_STATE:(g + 1) * SSD_STATE]
            xs_g = xs_ref[rows, gcols]
            st = st_ref[g]
            y_off = _dot(cm, st.astype(BF16)) * decay_out[:, gcols]
            new = _dot(bm_t, (xs_g * decay_in[:, gcols]).astype(BF16))
            st_ref[g] = st * decay_out[CHUNK - 1:CHUNK, gcols] + new
            y_diag = []
            for j in range(SSD_GROUP_WIDTH // LANES):
                pcols = slice(g * SSD_GROUP_WIDTH + j * LANES,
                              g * SSD_GROUP_WIDTH + (j + 1) * LANES)
                rhs = jnp.concatenate([xlo_ref[rows, pcols], xhi_ref[rows, pcols]], axis=0)
                y_diag.append(_dot(g_ref[c * (SSD_HEADS // 2) + g * 4 + j], rhs))
            y = jnp.concatenate(y_diag, axis=1) + y_off + xs_g * dskip_ref[:, gcols]
            ys.append(y * sz_ref[rows, gcols])
        y = jnp.concatenate(ys, axis=1)
        ymix_ref[rows, 0:SSD_WIDTH] = _rmsnorm(y, snw_ref[...]).astype(BF16)

    def ret_chunk(c):
        rows = pl.ds(c * CHUNK, CHUNK)
        for j in range(RET_QK_WIDTH // LANES):
            cols = slice(j * LANES, (j + 1) * LANES)
            vp = v_ref[rows, 2 * j * RET_V_DIM:(2 * j + 2) * RET_V_DIM]
            rp = r_ref[cols, :]
            rp_b = rp.astype(BF16)
            q2 = jnp.concatenate([qlo_ref[rows, cols], qhi_ref[rows, cols]], axis=0)
            sc = _dot(q2, kbt_ref[cols, rows])
            kv = _dot(kdt_ref[cols, rows], vp)
            r_ref[cols, :] = rp * cgam_ref[cols, :] + jnp.where(
                row_lo, kv[:, :RET_V_DIM], kv[:, RET_V_DIM:])
            for s in range(2):
                hd = 2 * j + s
                p = (sc[s * CHUNK:(s + 1) * CHUNK] * dmask_ref[hd]).astype(BF16)
                qm = (qdlo_ref if s == 0 else qdhi_ref)[rows, cols]
                o = _dot(jnp.concatenate([p, qm], axis=1),
                         jnp.concatenate([vp[:, s * RET_V_DIM:(s + 1) * RET_V_DIM], rp_b], axis=0))
                hcols = slice(hd * RET_V_DIM, (hd + 1) * RET_V_DIM)
                on = _rmsnorm(o, rnw_ref[:, hcols]) * sg_ref[rows, hcols]
                ymix_ref[rows, SSD_WIDTH + hd * RET_V_DIM:SSD_WIDTH + (hd + 1) * RET_V_DIM] = (
                    on.astype(BF16))

    projections = [project_z, project_q, project_k, project_v, project_g]
    for c in range(n_chunks):
        ssd_decay_matrices(c)
        if c < len(projections):
            projections[c]()
    for p in projections[n_chunks:]:
        p()
    for c in range(n_chunks):
        ssd_chunk(c)
        ret_chunk(c)

    y = _dot(ymix_ref[...], wout_ref[...])
    o_ref[0] = x + _rmsnorm(y, pw_ref[...])


def _ffn_kernel(x_ref, nw_ref, wup_ref, cw_ref, cb_ref, wdn_ref, pw_ref, o_ref,
                carry_ref, act_ref):
    t = x_ref.shape[1]

    @pl.when(pl.program_id(1) == 0)
    def _():
        carry_ref[...] = jnp.zeros_like(carry_ref)

    x = x_ref[0]
    h = _rmsnorm(x, nw_ref[...]).astype(BF16)
    for j in range(D_FF // MXU_COLS):
        lo = j * MXU_COLS
        cols = slice(lo, lo + MXU_COLS)
        gate = _dot(h, wup_ref[:, cols])
        val = _dot(h, wup_ref[:, D_FF + lo:D_FF + lo + MXU_COLS])
        d1, d2 = _shifted_rows(carry_ref[:, cols], gate, FFN_CONV - 1)
        cw = cw_ref[:, cols]
        conv = cw[2:3] * gate + cw[1:2] * d1 + cw[0:1] * d2 + cb_ref[:, cols]
        carry_ref[:, cols] = gate[t - SUBLANES:]
        act_ref[:, cols] = (_silu(conv) * val).astype(BF16)
    f = _dot(act_ref[...], wdn_ref[...])
    o_ref[0] = x + _rmsnorm(f, pw_ref[...])


def _resident(shape):
    return pl.BlockSpec(shape, lambda b, i: (0,) * len(shape),
                        pipeline_mode=pl.Buffered(1))


def _compiler_params():
    return pltpu.CompilerParams(dimension_semantics=("arbitrary", "arbitrary"),
                                vmem_limit_bytes=VMEM_LIMIT_BYTES)


def _retention_tables(seq):
    f64 = np.float64
    inv = ROPE_BASE ** (-np.arange(0, RET_QK_DIM, 2, dtype=f64) / RET_QK_DIM)
    ang = np.arange(seq, dtype=f64)[:, None] * inv[None, :]
    cos, sin = np.cos(ang), np.sin(ang)
    reps = LANES // RET_QK_DIM
    cos_t = np.tile(np.concatenate([cos, cos], axis=1), (1, reps))
    sin_t = np.tile(np.concatenate([-sin, sin], axis=1), (1, reps))
    log_gamma = np.log1p(-np.exp2(-5.0 - np.arange(RET_HEADS, dtype=f64)))
    pos = np.arange(CHUNK, dtype=f64)
    rel = pos[:, None] - pos[None, :]
    dmask = np.where(rel >= 0, np.exp(log_gamma[:, None, None] * np.maximum(rel, 0.0)), 0.0)
    k_decay = np.exp(log_gamma[:, None] * (CHUNK - 1 - pos)[None, :])
    q_decay = np.exp(log_gamma[:, None] * (pos + 1.0)[None, :])
    chunk_gamma = np.exp(log_gamma * CHUNK)
    kdec_t = np.repeat(k_decay, RET_QK_DIM, axis=0)
    qdec = np.repeat(q_decay.T, RET_QK_DIM, axis=1)
    cgam = np.broadcast_to(np.repeat(chunk_gamma, RET_QK_DIM)[:, None],
                           (RET_QK_WIDTH, RET_V_DIM))
    return tuple(jnp.asarray(a, dtype=F32) for a in (cos_t, sin_t, dmask, qdec, kdec_t, cgam))


def _pad_lanes(v):
    return jnp.pad(v, (0, LANES - v.shape[0])).reshape(1, LANES)


def _mixer(x, pre_w, w_in, conv_w, conv_b, dt_bias, a_log, d_skip, ssd_nw, ret_nw, w_out,
           post_w):
    bsz, seq, d = x.shape
    t = MIX_BLOCK_T
    w_bf = w_in.astype(BF16)
    w_b = w_bf[:, IN_B_START:]
    a_neg = _pad_lanes(-jnp.exp(a_log.astype(F32)))
    dskip = jnp.repeat(d_skip.astype(F32), SSD_HEAD_DIM).reshape(1, SSD_WIDTH)
    head_of_col = jnp.arange(SSD_WIDTH) // SSD_HEAD_DIM
    expand = (jnp.arange(LANES)[:, None] == head_of_col[None, :]).astype(BF16)
    expand2 = jnp.concatenate([expand, expand], axis=0)
    cos_t, sin_t, dmask, qdec, kdec_t, cgam = _retention_tables(seq)

    blk = lambda w: pl.BlockSpec((1, t, w), lambda b, i: (b, i, 0))
    return pl.pallas_call(
        _mixer_kernel,
        grid=(bsz, seq // t),
        in_specs=[
            blk(d),
            _resident((1, d)),
            _resident((d, IN_A_WIDTH)),
            _resident((d, IN_B_WIDTH)),
            _resident((SSD_CONV, SSD_XBC_WIDTH)),
            _resident((1, SSD_XBC_WIDTH)),
            _resident((1, LANES)),
            _resident((1, LANES)),
            _resident((1, SSD_WIDTH)),
            _resident((1, SSD_WIDTH)),
            _resident((1, RET_V_WIDTH)),
            _resident((MIX_WIDTH, d)),
            _resident((1, d)),
            pl.BlockSpec((t, LANES), lambda b, i: (i, 0)),
            pl.BlockSpec((t, LANES), lambda b, i: (i, 0)),
            _resident((2 * LANES, SSD_WIDTH)),
            _resident((RET_HEADS, CHUNK, CHUNK)),
            _resident((CHUNK, RET_QK_WIDTH)),
            _resident((RET_QK_WIDTH, CHUNK)),
            _resident((RET_QK_WIDTH, RET_V_DIM)),
        ],
        out_specs=blk(d),
        out_shape=jax.ShapeDtypeStruct(x.shape, F32),
        scratch_shapes=[
            pltpu.VMEM((SUBLANES, SSD_XBC_WIDTH), F32),
            pltpu.VMEM((t, SSD_WIDTH), F32),
            pltpu.VMEM((t, SSD_WIDTH), F32),
            pltpu.VMEM((t, SSD_WIDTH), BF16),
            pltpu.VMEM((t, SSD_WIDTH), BF16),
            pltpu.VMEM((SSD_GROUPS * SSD_STATE, t), BF16),
            pltpu.VMEM((t, SSD_GROUPS * SSD_STATE), BF16),
            pltpu.VMEM((t, LANES), F32),
            pltpu.VMEM((t, LANES), F32),
            pltpu.VMEM((t // CHUNK * SSD_HEADS // 2, CHUNK, 2 * CHUNK), BF16),
            pltpu.VMEM((t, RET_QK_WIDTH), BF16),
            pltpu.VMEM((t, RET_QK_WIDTH), BF16),
            pltpu.VMEM((t, RET_QK_WIDTH), BF16),
            pltpu.VMEM((t, RET_QK_WIDTH), BF16),
            pltpu.VMEM((RET_QK_WIDTH, t), BF16),
            pltpu.VMEM((RET_QK_WIDTH, t), BF16),
            pltpu.VMEM((t, RET_V_WIDTH), BF16),
            pltpu.VMEM((t, RET_V_WIDTH), F32),
            pltpu.VMEM((t, MIX_WIDTH), BF16),
            pltpu.VMEM((SSD_GROUPS, SSD_STATE, SSD_GROUP_WIDTH), F32),
            pltpu.VMEM((RET_QK_WIDTH, RET_V_DIM), F32),
        ],
        compiler_params=_compiler_params(),
        name="mixer",
    )(x, pre_w.reshape(1, d), w_bf, w_b, conv_w, conv_b.reshape(1, -1),
      _pad_lanes(dt_bias.astype(F32)), a_neg, dskip, ssd_nw.reshape(1, -1),
      ret_nw.reshape(1, -1), w_out.astype(BF16), post_w.reshape(1, d), cos_t, sin_t,
      expand2, dmask, qdec, kdec_t, cgam)


def _ffn(x, pre_w, w_up, conv_w, conv_b, w_down, post_w):
    bsz, seq, d = x.shape
    t = FFN_BLOCK_T
    blk = pl.BlockSpec((1, t, d), lambda b, i: (b, i, 0))
    return pl.pallas_call(
        _ffn_kernel,
        grid=(bsz, seq // t),
        in_specs=[
            blk,
            _resident((1, d)),
            _resident((d, 2 * D_FF)),
            _resident((FFN_CONV, D_FF)),
            _resident((1, D_FF)),
            _resident((D_FF, d)),
            _resident((1, d)),
        ],
        out_specs=blk,
        out_shape=jax.ShapeDtypeStruct(x.shape, F32),
        scratch_shapes=[
            pltpu.VMEM((SUBLANES, D_FF), F32),
            pltpu.VMEM((t, D_FF), BF16),
        ],
        compiler_params=_compiler_params(),
        name="conv_ffn",
    )(x, pre_w.reshape(1, d), w_up.astype(BF16), conv_w, conv_b.reshape(1, D_FF),
      w_down.astype(BF16), post_w.reshape(1, d))


def kernel(x, pre_mix_norm_w, w_in, ssd_conv_w, ssd_conv_b, ssd_dt_bias, ssd_a_log, ssd_d, ssd_norm_w, ret_norm_w, w_out, post_mix_norm_w, pre_ffn_norm_w, w_up, ffn_conv_w, ffn_conv_b, w_down, post_ffn_norm_w):
    for l in range(pre_mix_norm_w.shape[0]):
        x = _mixer(x, pre_mix_norm_w[l], w_in[l], ssd_conv_w[l], ssd_conv_b[l],
                   ssd_dt_bias[l], ssd_a_log[l], ssd_d[l], ssd_norm_w[l], ret_norm_w[l],
                   w_out[l], post_mix_norm_w[l])
        x = _ffn(x, pre_ffn_norm_w[l], w_up[l], ffn_conv_w[l], ffn_conv_b[l], w_down[l],
                 post_ffn_norm_w[l])
    return x
```

```python
import jax
import jax.numpy as jnp
import numpy as np
from jax import lax
from jax.experimental import pallas as pl
from jax.experimental.pallas import tpu as pltpu

D_MODEL = 1024
SSD_HEADS = 16
SSD_HEAD_DIM = 64
SSD_WIDTH = SSD_HEADS * SSD_HEAD_DIM
SSD_GROUPS = 2
SSD_STATE = 128
SSD_CONV = 4
SSD_BC_WIDTH = 2 * SSD_GROUPS * SSD_STATE
SSD_XBC_WIDTH = SSD_WIDTH + SSD_BC_WIDTH
SSD_GROUP_WIDTH = SSD_WIDTH // SSD_GROUPS
RET_HEADS = 8
RET_QK_DIM = 64
RET_V_DIM = 128
RET_QK_WIDTH = RET_HEADS * RET_QK_DIM
RET_V_WIDTH = RET_HEADS * RET_V_DIM
ROPE_BASE = 10000.0
CHUNK = 128
MIX_WIDTH = SSD_WIDTH + RET_V_WIDTH
D_FF = 2816
FFN_CONV = 3
EPS = 1e-6

SUBLANES = 8
LANES = 128
MXU_COLS = 256
MIX_BLOCK_T = 512
FFN_BLOCK_T = 512
CONV_COL_TILE = 512
VMEM_LIMIT_BYTES = 56 * 1024 * 1024

OFF_Z = 0
OFF_XBC = OFF_Z + SSD_WIDTH
OFF_DT = OFF_XBC + SSD_XBC_WIDTH
IN_A_WIDTH = OFF_DT + LANES
IN_B_START = OFF_DT + SSD_HEADS
OFF_Q = 0
OFF_K = OFF_Q + RET_QK_WIDTH
OFF_V = OFF_K + RET_QK_WIDTH
OFF_G = OFF_V + RET_V_WIDTH
IN_B_WIDTH = OFF_G + RET_V_WIDTH

F32 = jnp.float32
BF16 = jnp.bfloat16


def _rmsnorm(x, w):
    return x * lax.rsqrt(jnp.mean(x * x, axis=-1, keepdims=True) + EPS) * w


LOG2_E = 1.4426950408889634


def _silu(x):
    return x * (1.0 / (1.0 + jnp.exp2(x * (-LOG2_E))))


def _softplus(x):
    return jnp.maximum(x, 0.0) + jnp.log1p(jnp.exp(-jnp.abs(x)))


def _dot(a, b):
    return jnp.dot(a, b, preferred_element_type=F32)


def _split_bf16(x):
    hi = x.astype(BF16)
    lo = (x - hi.astype(F32)).astype(BF16)
    return hi, lo


def _shifted_rows(carry, cur, n_shifts):
    cat = jnp.concatenate([carry, cur], axis=0)
    return [pltpu.roll(cat, s, 0)[SUBLANES:] for s in range(1, n_shifts + 1)]


def _mixer_kernel(x_ref, nw_ref, wina_ref, winb_ref, cw_ref, cb_ref, dtb_ref, a_ref, dskip_ref,
                  snw_ref, rnw_ref, wout_ref, pw_ref, cos_ref, sin_ref, expand_ref,
                  dmask_ref, qdec_ref, kdect_ref, cgam_ref, o_ref,
                  carry_ref, sz_ref, xs_ref, xlo_ref, xhi_ref, bmt_ref, cm_ref, dt_ref, acs_ref,
                  g_ref, qlo_ref, qhi_ref, qdlo_ref, qdhi_ref, kbt_ref, kdt_ref, v_ref, sg_ref,
                  ymix_ref, st_ref, r_ref):
    t = x_ref.shape[1]
    n_chunks = t // CHUNK

    @pl.when(pl.program_id(1) == 0)
    def _():
        carry_ref[...] = jnp.zeros_like(carry_ref)
        st_ref[...] = jnp.zeros_like(st_ref)
        r_ref[...] = jnp.zeros_like(r_ref)

    row = lax.broadcasted_iota(jnp.int32, (CHUNK, CHUNK), 0)
    col = lax.broadcasted_iota(jnp.int32, (CHUNK, CHUNK), 1)
    causal = row >= col
    tril2 = jnp.where(jnp.concatenate([causal, causal], axis=1), 1.0, 0.0).astype(BF16)
    row_lo = row < RET_QK_DIM

    def pair_masks(width):
        lane = lax.broadcasted_iota(jnp.int32, (t, width), 1)
        return (lane % LANES) < (LANES // 2)

    x = x_ref[0]
    h = _rmsnorm(x, nw_ref[...]).astype(BF16)
    dt_ref[...] = _softplus(_dot(h, wina_ref[:, OFF_DT:OFF_DT + LANES]) + dtb_ref[...])
    lo_c = pair_masks(CONV_COL_TILE)
    for j in range(SSD_XBC_WIDTH // CONV_COL_TILE):
        lo = j * CONV_COL_TILE
        cols = slice(lo, lo + CONV_COL_TILE)
        cur = _dot(h, wina_ref[:, OFF_XBC + lo:OFF_XBC + lo + CONV_COL_TILE])
        d1, d2, d3 = _shifted_rows(carry_ref[:, cols], cur, SSD_CONV - 1)
        cw = cw_ref[:, cols]
        conv = (cw[3:4] * cur + cw[2:3] * d1 + cw[1:2] * d2 + cw[0:1] * d3
                + cb_ref[:, cols])
        carry_ref[:, cols] = cur[t - SUBLANES:]
        act = _silu(conv)
        if lo < SSD_WIDTH:
            xs_ref[:, cols] = act
            xlo_ref[:, cols] = jnp.where(lo_c, act, 0.0).astype(BF16)
            xhi_ref[:, cols] = jnp.where(lo_c, 0.0, act).astype(BF16)
        else:
            bmt_ref[...] = act[:, :SSD_GROUPS * SSD_STATE].T.astype(BF16)
            cm_ref[...] = act[:, SSD_GROUPS * SSD_STATE:].astype(BF16)

    def expand_heads(w):
        hi, lo = _split_bf16(w)
        return _dot(jnp.concatenate([hi, lo], axis=1), expand_ref[...])

    def ssd_decay_matrices(c):
        rows = pl.ds(c * CHUNK, CHUNK)
        dt = dt_ref[rows, :]
        da = dt * a_ref[...]
        da_hi, da_lo = _split_bf16(da)
        acs = _dot(tril2, jnp.concatenate([da_hi, da_lo], axis=0))
        acs_ref[rows, :] = acs
        acs_t = acs.T
        dt_t = dt.T
        for g in range(SSD_GROUPS):
            bm_t = bmt_ref[g * SSD_STATE:(g + 1) * SSD_STATE, rows]
            cm = cm_ref[rows, g * SSD_STATE:(g + 1) * SSD_STATE]
            cb = _dot(cm, bm_t)
            for j in range(SSD_GROUP_WIDTH // LANES):
                for s in range(2):
                    k = g * 8 + 2 * j + s
                    seg = acs[:, k:k + 1] - acs_t[k:k + 1, :]
                    lmat = jnp.exp(jnp.where(causal, seg, -jnp.inf))
                    g_ref[c * (SSD_HEADS // 2) + g * 4 + j, :, s * CHUNK:(s + 1) * CHUNK] = (
                        (cb * lmat * dt_t[k:k + 1, :]).astype(BF16))

    first_half = (lax.broadcasted_iota(jnp.int32, (t, RET_QK_WIDTH), 1) % RET_QK_DIM
                  < RET_QK_DIM // 2)
    cos = jnp.concatenate([cos_ref[...]] * (RET_QK_WIDTH // LANES), axis=1)
    sin = jnp.concatenate([sin_ref[...]] * (RET_QK_WIDTH // LANES), axis=1)
    lo_q = pair_masks(RET_QK_WIDTH)

    def rotary(u):
        swapped = jnp.where(first_half,
                            pltpu.roll(u, RET_QK_WIDTH - RET_QK_DIM // 2, 1),
                            pltpu.roll(u, RET_QK_DIM // 2, 1))
        return u * cos + swapped * sin

    def project_z():
        sz_ref[...] = _silu(_dot(h, wina_ref[:, OFF_Z:OFF_Z + SSD_WIDTH]))

    def project_q():
        q = rotary(_dot(h, winb_ref[:, OFF_Q:OFF_Q + RET_QK_WIDTH]))
        qd = q * jnp.concatenate([qdec_ref[...]] * n_chunks, axis=0)
        qlo_ref[...] = jnp.where(lo_q, q, 0.0).astype(BF16)
        qhi_ref[...] = jnp.where(lo_q, 0.0, q).astype(BF16)
        qdlo_ref[...] = jnp.where(lo_q, qd, 0.0).astype(BF16)
        qdhi_ref[...] = jnp.where(lo_q, 0.0, qd).astype(BF16)

    def project_k():
        k = rotary(_dot(h, winb_ref[:, OFF_K:OFF_K + RET_QK_WIDTH])) * (RET_QK_DIM ** -0.5)
        k_t = k.T
        kbt_ref[...] = k_t.astype(BF16)
        kdt_ref[...] = (k_t * jnp.concatenate([kdect_ref[...]] * n_chunks, axis=1)).astype(BF16)

    def project_v():
        v_ref[...] = _dot(h, winb_ref[:, OFF_V:OFF_V + RET_V_WIDTH]).astype(BF16)

    def project_g():
        sg_ref[...] = _silu(_dot(h, winb_ref[:, OFF_G:OFF_G + RET_V_WIDTH]))

    def ssd_decays(c):
        rows = pl.ds(c * CHUNK, CHUNK)
        dt = dt_ref[rows, :]
        acs = acs_ref[rows, :]
        acs_last = acs[CHUNK - 1:CHUNK, :]
        decay_in = expand_heads(dt * jnp.exp(acs_last - acs))
        decay_out = expand_heads(jnp.exp(acs))
        return decay_in, decay_out

    def ssd_state(c, g, decay_in, decay_out):
        rows = pl.ds(c * CHUNK, CHUNK)
        gcols = slice(g * SSD_GROUP_WIDTH, (g + 1) * SSD_GROUP_WIDTH)
        bm_t = bmt_ref[g * SSD_STATE:(g + 1) * SSD_STATE, rows]
        cm = cm_ref[rows, g * SSD_STATE:(g + 1) * SSD_STATE]
        xs_g = xs_ref[rows, gcols]
        st = st_ref[g]
        y_off = _dot(cm, st.astype(BF16)) * decay_out[:, gcols]
        new = _dot(bm_t, (xs_g * decay_in[:, gcols]).astype(BF16))
        st_ref[g] = st * decay_out[CHUNK - 1:CHUNK, gcols] + new
        return y_off + xs_g * dskip_ref[:, gcols]

    def ssd_diag(c, g, y_rest):
        rows = pl.ds(c * CHUNK, CHUNK)
        gcols = slice(g * SSD_GROUP_WIDTH, (g + 1) * SSD_GROUP_WIDTH)
        y_diag = []
        for j in range(SSD_GROUP_WIDTH // LANES):
            pcols = slice(g * SSD_GROUP_WIDTH + j * LANES,
                          g * SSD_GROUP_WIDTH + (j + 1) * LANES)
            rhs = jnp.concatenate([xlo_ref[rows, pcols], xhi_ref[rows, pcols]], axis=0)
            y_diag.append(_dot(g_ref[c * (SSD_HEADS // 2) + g * 4 + j], rhs))
        return (jnp.concatenate(y_diag, axis=1) + y_rest) * sz_ref[rows, gcols]

    def ssd_finish(c, ys):
        rows = pl.ds(c * CHUNK, CHUNK)
        y = jnp.concatenate(ys, axis=1)
        ymix_ref[rows, 0:SSD_WIDTH] = _rmsnorm(y, snw_ref[...]).astype(BF16)

    def ret_pair(c, j):
        rows = pl.ds(c * CHUNK, CHUNK)
        cols = slice(j * LANES, (j + 1) * LANES)
        vp = v_ref[rows, 2 * j * RET_V_DIM:(2 * j + 2) * RET_V_DIM]
        rp = r_ref[cols, :]
        rp_b = rp.astype(BF16)
        q2 = jnp.concatenate([qlo_ref[rows, cols], qhi_ref[rows, cols]], axis=0)
        sc = _dot(q2, kbt_ref[cols, rows])
        kv = _dot(kdt_ref[cols, rows], vp)
        r_ref[cols, :] = rp * cgam_ref[cols, :] + jnp.where(
            row_lo, kv[:, :RET_V_DIM], kv[:, RET_V_DIM:])
        for s in range(2):
            hd = 2 * j + s
            p = (sc[s * CHUNK:(s + 1) * CHUNK] * dmask_ref[hd]).astype(BF16)
            qm = (qdlo_ref if s == 0 else qdhi_ref)[rows, cols]
            o = _dot(jnp.concatenate([p, qm], axis=1),
                     jnp.concatenate([vp[:, s * RET_V_DIM:(s + 1) * RET_V_DIM], rp_b], axis=0))
            hcols = slice(hd * RET_V_DIM, (hd + 1) * RET_V_DIM)
            on = _rmsnorm(o, rnw_ref[:, hcols]) * sg_ref[rows, hcols]
            ymix_ref[rows, SSD_WIDTH + hd * RET_V_DIM:SSD_WIDTH + (hd + 1) * RET_V_DIM] = (
                on.astype(BF16))

    def mix_chunk(c):
        decay_in, decay_out = ssd_decays(c)
        ys = []
        for g in range(SSD_GROUPS):
            y_rest = ssd_state(c, g, decay_in, decay_out)
            ret_pair(c, 2 * g)
            ys.append(ssd_diag(c, g, y_rest))
            ret_pair(c, 2 * g + 1)
        ssd_finish(c, ys)

    projections = [project_z, project_q, project_k, project_v, project_g]
    for c in range(n_chunks):
        ssd_decay_matrices(c)
        if c < len(projections):
            projections[c]()
    for p in projections[n_chunks:]:
        p()
    for c in range(n_chunks):
        mix_chunk(c)

    y = _dot(ymix_ref[...], wout_ref[...])
    o_ref[0] = x + _rmsnorm(y, pw_ref[...])


def _ffn_kernel(x_ref, nw_ref, wup_ref, cw_ref, cb_ref, wdn_ref, pw_ref, o_ref,
                carry_ref, act_ref):
    t = x_ref.shape[1]

    @pl.when(pl.program_id(1) == 0)
    def _():
        carry_ref[...] = jnp.zeros_like(carry_ref)

    x = x_ref[0]
    h = _rmsnorm(x, nw_ref[...]).astype(BF16)
    for j in range(D_FF // MXU_COLS):
        lo = j * MXU_COLS
        cols = slice(lo, lo + MXU_COLS)
        gate = _dot(h, wup_ref[:, cols])
        val = _dot(h, wup_ref[:, D_FF + lo:D_FF + lo + MXU_COLS])
        d1, d2 = _shifted_rows(carry_ref[:, cols], gate, FFN_CONV - 1)
        cw = cw_ref[:, cols]
        conv = cw[2:3] * gate + cw[1:2] * d1 + cw[0:1] * d2 + cb_ref[:, cols]
        carry_ref[:, cols] = gate[t - SUBLANES:]
        act_ref[:, cols] = (_silu(conv) * val).astype(BF16)
    f = _dot(act_ref[...], wdn_ref[...])
    o_ref[0] = x + _rmsnorm(f, pw_ref[...])


def _resident(shape):
    return pl.BlockSpec(shape, lambda b, i: (0,) * len(shape),
                        pipeline_mode=pl.Buffered(1))


def _compiler_params():
    return pltpu.CompilerParams(dimension_semantics=("arbitrary", "arbitrary"),
                                vmem_limit_bytes=VMEM_LIMIT_BYTES)


def _retention_tables(seq):
    f64 = np.float64
    inv = ROPE_BASE ** (-np.arange(0, RET_QK_DIM, 2, dtype=f64) / RET_QK_DIM)
    ang = np.arange(seq, dtype=f64)[:, None] * inv[None, :]
    cos, sin = np.cos(ang), np.sin(ang)
    reps = LANES // RET_QK_DIM
    cos_t = np.tile(np.concatenate([cos, cos], axis=1), (1, reps))
    sin_t = np.tile(np.concatenate([-sin, sin], axis=1), (1, reps))
    log_gamma = np.log1p(-np.exp2(-5.0 - np.arange(RET_HEADS, dtype=f64)))
    pos = np.arange(CHUNK, dtype=f64)
    rel = pos[:, None] - pos[None, :]
    dmask = np.where(rel >= 0, np.exp(log_gamma[:, None, None] * np.maximum(rel, 0.0)), 0.0)
    k_decay = np.exp(log_gamma[:, None] * (CHUNK - 1 - pos)[None, :])
    q_decay = np.exp(log_gamma[:, None] * (pos + 1.0)[None, :])
    chunk_gamma = np.exp(log_gamma * CHUNK)
    kdec_t = np.repeat(k_decay, RET_QK_DIM, axis=0)
    qdec = np.repeat(q_decay.T, RET_QK_DIM, axis=1)
    cgam = np.broadcast_to(np.repeat(chunk_gamma, RET_QK_DIM)[:, None],
                           (RET_QK_WIDTH, RET_V_DIM))
    return tuple(jnp.asarray(a, dtype=F32) for a in (cos_t, sin_t, dmask, qdec, kdec_t, cgam))


def _pad_lanes(v):
    return jnp.pad(v, (0, LANES - v.shape[0])).reshape(1, LANES)


def _mixer(x, pre_w, w_in, conv_w, conv_b, dt_bias, a_log, d_skip, ssd_nw, ret_nw, w_out,
           post_w):
    bsz, seq, d = x.shape
    t = MIX_BLOCK_T
    w_bf = w_in.astype(BF16)
    w_b = w_bf[:, IN_B_START:]
    a_neg = _pad_lanes(-jnp.exp(a_log.astype(F32)))
    dskip = jnp.repeat(d_skip.astype(F32), SSD_HEAD_DIM).reshape(1, SSD_WIDTH)
    head_of_col = jnp.arange(SSD_WIDTH) // SSD_HEAD_DIM
    expand = (jnp.arange(LANES)[:, None] == head_of_col[None, :]).astype(BF16)
    expand2 = jnp.concatenate([expand, expand], axis=0)
    cos_t, sin_t, dmask, qdec, kdec_t, cgam = _retention_tables(seq)

    blk = lambda w: pl.BlockSpec((1, t, w), lambda b, i: (b, i, 0))
    return pl.pallas_call(
        _mixer_kernel,
        grid=(bsz, seq // t),
        in_specs=[
            blk(d),
            _resident((1, d)),
            _resident((d, IN_A_WIDTH)),
            _resident((d, IN_B_WIDTH)),
            _resident((SSD_CONV, SSD_XBC_WIDTH)),
            _resident((1, SSD_XBC_WIDTH)),
            _resident((1, LANES)),
            _resident((1, LANES)),
            _resident((1, SSD_WIDTH)),
            _resident((1, SSD_WIDTH)),
            _resident((1, RET_V_WIDTH)),
            _resident((MIX_WIDTH, d)),
            _resident((1, d)),
            pl.BlockSpec((t, LANES), lambda b, i: (i, 0)),
            pl.BlockSpec((t, LANES), lambda b, i: (i, 0)),
            _resident((2 * LANES, SSD_WIDTH)),
            _resident((RET_HEADS, CHUNK, CHUNK)),
            _resident((CHUNK, RET_QK_WIDTH)),
            _resident((RET_QK_WIDTH, CHUNK)),
            _resident((RET_QK_WIDTH, RET_V_DIM)),
        ],
        out_specs=blk(d),
        out_shape=jax.ShapeDtypeStruct(x.shape, F32),
        scratch_shapes=[
            pltpu.VMEM((SUBLANES, SSD_XBC_WIDTH), F32),
            pltpu.VMEM((t, SSD_WIDTH), F32),
            pltpu.VMEM((t, SSD_WIDTH), F32),
            pltpu.VMEM((t, SSD_WIDTH), BF16),
            pltpu.VMEM((t, SSD_WIDTH), BF16),
            pltpu.VMEM((SSD_GROUPS * SSD_STATE, t), BF16),
            pltpu.VMEM((t, SSD_GROUPS * SSD_STATE), BF16),
            pltpu.VMEM((t, LANES), F32),
            pltpu.VMEM((t, LANES), F32),
            pltpu.VMEM((t // CHUNK * SSD_HEADS // 2, CHUNK, 2 * CHUNK), BF16),
            pltpu.VMEM((t, RET_QK_WIDTH), BF16),
            pltpu.VMEM((t, RET_QK_WIDTH), BF16),
            pltpu.VMEM((t, RET_QK_WIDTH), BF16),
            pltpu.VMEM((t, RET_QK_WIDTH), BF16),
            pltpu.VMEM((RET_QK_WIDTH, t), BF16),
            pltpu.VMEM((RET_QK_WIDTH, t), BF16),
            pltpu.VMEM((t, RET_V_WIDTH), BF16),
            pltpu.VMEM((t, RET_V_WIDTH), F32),
            pltpu.VMEM((t, MIX_WIDTH), BF16),
            pltpu.VMEM((SSD_GROUPS, SSD_STATE, SSD_GROUP_WIDTH), F32),
            pltpu.VMEM((RET_QK_WIDTH, RET_V_DIM), F32),
        ],
        compiler_params=_compiler_params(),
        name="mixer",
    )(x, pre_w.reshape(1, d), w_bf, w_b, conv_w, conv_b.reshape(1, -1),
      _pad_lanes(dt_bias.astype(F32)), a_neg, dskip, ssd_nw.reshape(1, -1),
      ret_nw.reshape(1, -1), w_out.astype(BF16), post_w.reshape(1, d), cos_t, sin_t,
      expand2, dmask, qdec, kdec_t, cgam)


def _ffn(x, pre_w, w_up, conv_w, conv_b, w_down, post_w):
    bsz, seq, d = x.shape
    t = FFN_BLOCK_T
    blk = pl.BlockSpec((1, t, d), lambda b, i: (b, i, 0))
    return pl.pallas_call(
        _ffn_kernel,
        grid=(bsz, seq // t),
        in_specs=[
            blk,
            _resident((1, d)),
            _resident((d, 2 * D_FF)),
            _resident((FFN_CONV, D_FF)),
            _resident((1, D_FF)),
            _resident((D_FF, d)),
            _resident((1, d)),
        ],
        out_specs=blk,
        out_shape=jax.ShapeDtypeStruct(x.shape, F32),
        scratch_shapes=[
            pltpu.VMEM((SUBLANES, D_FF), F32),
            pltpu.VMEM((t, D_FF), BF16),
        ],
        compiler_params=_compiler_params(),
        name="conv_ffn",
    )(x, pre_w.reshape(1, d), w_up.astype(BF16), conv_w, conv_b.reshape(1, D_FF),
      w_down.astype(BF16), post_w.reshape(1, d))


def kernel(x, pre_mix_norm_w, w_in, ssd_conv_w, ssd_conv_b, ssd_dt_bias, ssd_a_log, ssd_d, ssd_norm_w, ret_norm_w, w_out, post_mix_norm_w, pre_ffn_norm_w, w_up, ffn_conv_w, ffn_conv_b, w_down, post_ffn_norm_w):
    for l in range(pre_mix_norm_w.shape[0]):
        x = _mixer(x, pre_mix_norm_w[l], w_in[l], ssd_conv_w[l], ssd_conv_b[l],
                   ssd_dt_bias[l], ssd_a_log[l], ssd_d[l], ssd_norm_w[l], ret_norm_w[l],
                   w_out[l], post_mix_norm_w[l])
        x = _ffn(x, pre_ffn_norm_w[l], w_up[l], ffn_conv_w[l], ffn_conv_b[l], w_down[l],
                 post_ffn_norm_w[l])
    return x
```
